```python
import math
import jax, jax.numpy as jnp
from jax import lax
import numpy as np

D_MODEL = 1024
BATCH = 8
SEQ = 8192
DEPTH = 2

N_META = 16
BLOCK = 128
EPS = 1e-6

SSD_HEADS = 8
SSD_HEAD_DIM = 64
SSD_WIDTH = SSD_HEADS * SSD_HEAD_DIM
SSD_GROUPS = 2
SSD_HEADS_PER_GROUP = SSD_HEADS // SSD_GROUPS
SSD_STATE = 128
SSD_CONV = 4
SSD_XBC = SSD_WIDTH + 2 * SSD_GROUPS * SSD_STATE

SB_HEADS = 4
SB_HEAD_DIM = 64
SB_WIDTH = SB_HEADS * SB_HEAD_DIM

MLA_HEADS = 4
MLA_NOPE = 64
MLA_ROPE = 32
MLA_V = 64
MLA_WIDTH = MLA_HEADS * MLA_V
MLA_Q_RANK = 192
MLA_KV_RANK = 128
ROPE_BASE = 10000.0

MIX_WIDTH = SSD_WIDTH + SB_WIDTH + MLA_WIDTH
IN_SIZES = (SSD_WIDTH, SSD_XBC, SSD_HEADS, SB_WIDTH, SB_WIDTH, SB_WIDTH, MLA_Q_RANK, MLA_KV_RANK, MLA_ROPE)
IN_COLS = 2664

D_FF = 2816
FFN_CONV = 3

kernel_name = "hymba_ssd_stickbreak_mla_convffn"


def rmsnorm(x, g):
    xf = x.astype(jnp.float32)
    xf = xf * lax.rsqrt(jnp.mean(xf * xf, axis=-1, keepdims=True) + EPS)
    return xf.astype(x.dtype) * g


def causal_dwconv(u, w):
    k_w, c = w.shape
    return lax.conv_general_dilated(
        u, w.reshape(k_w, 1, c).astype(u.dtype), window_strides=(1,), padding=[(k_w - 1, 0)],
        dimension_numbers=("NWC", "WIO", "NWC"), feature_group_count=c)


def rope_tables(length, dtype):
    pos = jnp.arange(length, dtype=jnp.float32)
    inv = 1.0 / (ROPE_BASE ** (jnp.arange(0, MLA_ROPE, 2, dtype=jnp.float32) / MLA_ROPE))
    ang = pos[:, None] * inv[None, :]
    ang = jnp.concatenate([ang, ang], axis=-1)
    return jnp.cos(ang).astype(dtype), jnp.sin(ang).astype(dtype)


def apply_rope(x, cos, sin):
    half = MLA_ROPE // 2
    rot = jnp.concatenate([-x[..., half:], x[..., :half]], axis=-1)
    return x * cos[None, :, None, :] + rot * sin[None, :, None, :]


def sweep_query_blocks(q, block_fn):
    bsz, length = q.shape[0], q.shape[1]
    pad = BLOCK - N_META
    qp = jnp.pad(q, ((0, 0), (pad, 0), (0, 0), (0, 0)))
    nb = qp.shape[1] // BLOCK
    qb = jnp.moveaxis(qp.reshape(bsz, nb, BLOCK, q.shape[2], q.shape[3]), 1, 0)
    starts = jnp.arange(nb, dtype=jnp.int32) * BLOCK - pad
    offs = jnp.arange(BLOCK, dtype=jnp.int32)
    out = lax.map(lambda a: block_fn(a[0], a[1] + offs), (qb, starts))
    out = jnp.moveaxis(out, 0, 1).reshape(bsz, nb * BLOCK, -1)
    return out[:, pad:]


def ssd_mixer(z, xbc, dt_raw, conv_w, conv_b, dt_bias, a_log, d_skip, norm_g):
    f32 = jnp.float32
    bsz, length, _ = z.shape
    G, R, P, N = SSD_GROUPS, SSD_HEADS_PER_GROUP, SSD_HEAD_DIM, SSD_STATE
    gn = G * N
    xbc = jax.nn.silu(causal_dwconv(xbc, conv_w) + conv_b)
    xs = xbc[..., :SSD_WIDTH].reshape(bsz, length, SSD_HEADS, P).astype(f32)
    b_in = xbc[..., SSD_WIDTH:SSD_WIDTH + gn].reshape(bsz, length, G, N).astype(f32)
    c_in = xbc[..., SSD_WIDTH + gn:].reshape(bsz, length, G, N).astype(f32)
    dt = jax.nn.softplus((dt_raw + dt_bias).astype(f32))
    a = -jnp.exp(a_log.astype(f32))

    pad = BLOCK - N_META
    nc = (length + pad) // BLOCK

    def chunked(t):
        t = jnp.pad(t, ((0, 0), (pad, 0)) + ((0, 0),) * (t.ndim - 2))
        return t.reshape((bsz, nc, BLOCK) + t.shape[2:])

    X = chunked(xs * dt[..., None]).reshape(bsz, nc, BLOCK, G, R, P)
    dA = chunked(dt * a).reshape(bsz, nc, BLOCK, G, R)
    Bc = chunked(b_in)
    Cc = chunked(c_in)
    a_cs = jnp.cumsum(dA, axis=2)

    causal = jnp.tril(jnp.ones((BLOCK, BLOCK), dtype=bool))
    seg = a_cs[:, :, :, None] - a_cs[:, :, None, :]
    decay_ls = jnp.exp(jnp.where(causal[:, :, None, None], seg, -jnp.inf))
    cb = jnp.einsum('bclgn,bcsgn->bclsg', Cc, Bc)
    y_diag = jnp.einsum('bclsg,bclsgr,bcsgrp->bclgrp', cb, decay_ls, X)

    decay_to_end = jnp.exp(a_cs[:, :, -1:] - a_cs)
    states = jnp.einsum('bclgn,bclgr,bclgrp->bcgrpn', Bc, decay_to_end, X)
    chunk_decay = jnp.exp(a_cs[:, :, -1])

    def step(h, inp):
        st, dec = inp
        return h * dec[..., None, None] + st, h

    h0 = jnp.zeros((bsz, G, R, P, N), f32)
    _, h_prev = lax.scan(step, h0, (jnp.moveaxis(states, 1, 0), jnp.moveaxis(chunk_decay, 1, 0)))
    h_prev = jnp.moveaxis(h_prev, 0, 1)
    y_off = jnp.einsum('bclgn,bcgrpn,bclgr->bclgrp', Cc, h_prev, jnp.exp(a_cs))

    y = (y_diag + y_off).reshape(bsz, nc * BLOCK, SSD_WIDTH)[:, pad:]
    y = y + (xs * d_skip.astype(f32)[:, None]).reshape(bsz, length, SSD_WIDTH)
    y = y.astype(z.dtype) * jax.nn.silu(z)
    return rmsnorm(y, norm_g)


def stick_breaking_attention(q, k, v):
    length = k.shape[1]
    key_pos = jnp.arange(length, dtype=jnp.int32)
    scale = SB_HEAD_DIM ** -0.5

    def block_fn(qi, qpos):
        zs = jnp.einsum('bqhd,bkhd->bhqk', qi, k).astype(jnp.float32) * scale
        mask = key_pos[None, :] < qpos[:, None]
        u = jnp.where(mask, jax.nn.log_sigmoid(-zs), 0.0)
        after = lax.cumsum(u, axis=3, reverse=True) - u
        w = jnp.where(mask, jnp.exp(jax.nn.log_sigmoid(zs) + after), 0.0)
        return jnp.einsum('bhqk,bkhd->bqhd', w.astype(v.dtype), v)

    return sweep_query_blocks(q, block_fn)


def mla_attention(q_a, c_kv, k_r, q_norm_g, kv_norm_g, w_uq, w_ukv, cos, sin):
    bsz, length, _ = q_a.shape
    q = (rmsnorm(q_a, q_norm_g) @ w_uq).reshape(bsz, length, MLA_HEADS, MLA_NOPE + MLA_ROPE)
    q = jnp.concatenate([q[..., :MLA_NOPE], apply_rope(q[..., MLA_NOPE:], cos, sin)], axis=-1)
    kv = (rmsnorm(c_kv, kv_norm_g) @ w_ukv).reshape(bsz, length, MLA_HEADS, MLA_NOPE + MLA_V)
    k_rope = apply_rope(k_r[:, :, None, :], cos, sin)
    k = jnp.concatenate([kv[..., :MLA_NOPE],
                         jnp.broadcast_to(k_rope, (bsz, length, MLA_HEADS, MLA_ROPE))], axis=-1)
    v = kv[..., MLA_NOPE:]
    key_pos = jnp.arange(length, dtype=jnp.int32)
    scale = (MLA_NOPE + MLA_ROPE) ** -0.5

    def block_fn(qi, qpos):
        s = jnp.einsum('bqhd,bkhd->bhqk', qi, k).astype(jnp.float32) * scale
        mask = key_pos[None, :] <= qpos[:, None]
        p = jax.nn.softmax(jnp.where(mask, s, -1e30), axis=-1)
        return jnp.einsum('bhqk,bkhd->bqhd', p.astype(v.dtype), v)

    return sweep_query_blocks(q, block_fn)


def conv_ffn(h, w_up, conv_w, conv_b, w_down):
    u = causal_dwconv(h @ w_up, conv_w) + conv_b
    return (jax.nn.silu(u[..., :D_FF]) * u[..., D_FF:]) @ w_down


def setup_inputs(seed: int = 0) -> dict:
    key = jax.random.key(seed)
    ks = jax.random.split(key, 24)
    f32 = jnp.float32
    nrm = lambda k, shape, s: jax.random.normal(k, shape, f32) * s
    gain = lambda k, shape: 1.0 + 0.05 * jax.random.normal(k, shape, f32)
    dt = jnp.exp(jax.random.uniform(ks[5], (DEPTH, SSD_HEADS), f32) * (math.log(0.1) - math.log(1e-3)) + math.log(1e-3))
    return {
        "x": nrm(ks[0], (BATCH, SEQ, D_MODEL), 1.0),
        "meta_tokens": nrm(ks[1], (N_META, D_MODEL), 1.0),
        "norm_mix_g": gain(ks[2], (DEPTH, D_MODEL)),
        "w_in": nrm(ks[3], (DEPTH, D_MODEL, IN_COLS), D_MODEL ** -0.5),
        "ssd_conv_w": nrm(ks[4], (DEPTH, SSD_CONV, SSD_XBC), SSD_CONV ** -0.5),
        "ssd_conv_b": nrm(ks[6], (DEPTH, SSD_XBC), 0.02),
        "ssd_dt_bias": dt + jnp.log(-jnp.expm1(-dt)),
        "ssd_a_log": jnp.log(jax.random.uniform(ks[7], (DEPTH, SSD_HEADS), f32, 1.0, 16.0)),
        "ssd_d": gain(ks[8], (DEPTH, SSD_HEADS)),
        "ssd_norm_g": gain(ks[9], (DEPTH, SSD_WIDTH)),
        "sb_norm_g": gain(ks[10], (DEPTH, SB_WIDTH)),
        "mla_q_norm_g": gain(ks[11], (DEPTH, MLA_Q_RANK)),
        "mla_kv_norm_g": gain(ks[12], (DEPTH, MLA_KV_RANK)),
        "mla_w_uq": nrm(ks[13], (DEPTH, MLA_Q_RANK, MLA_HEADS * (MLA_NOPE + MLA_ROPE)), MLA_Q_RANK ** -0.5),
        "mla_w_ukv": nrm(ks[14], (DEPTH, MLA_KV_RANK, MLA_HEADS * (MLA_NOPE + MLA_V)), MLA_KV_RANK ** -0.5),
        "mla_norm_g": gain(ks[15], (DEPTH, MLA_WIDTH)),
        "w_out": nrm(ks[16], (DEPTH, MIX_WIDTH, D_MODEL), MIX_WIDTH ** -0.5),
        "norm_ffn_g": gain(ks[17], (DEPTH, D_MODEL)),
        "ffn_w_up": nrm(ks[18], (DEPTH, D_MODEL, 2 * D_FF), D_MODEL ** -0.5),
        "ffn_conv_w": nrm(ks[19], (DEPTH, FFN_CONV, 2 * D_FF), FFN_CONV ** -0.5),
        "ffn_conv_b": nrm(ks[20], (DEPTH, 2 * D_FF), 0.02),
        "ffn_w_down": nrm(ks[21], (DEPTH, D_FF, D_MODEL), D_FF ** -0.5),
        "final_norm_g": gain(ks[22], (D_MODEL,)),
    }


def reference(x, meta_tokens, norm_mix_g, w_in, ssd_conv_w, ssd_conv_b, ssd_dt_bias, ssd_a_log,
              ssd_d, ssd_norm_g, sb_norm_g, mla_q_norm_g, mla_kv_norm_g, mla_w_uq, mla_w_ukv,
              mla_norm_g, w_out, norm_ffn_g, ffn_w_up, ffn_conv_w, ffn_conv_b, ffn_w_down,
              final_norm_g):
    bsz = x.shape[0]
    meta = jnp.broadcast_to(meta_tokens[None].astype(x.dtype), (bsz, N_META, x.shape[2]))
    h = jnp.concatenate([meta, x], axis=1)
    length = h.shape[1]
    cos, sin = rope_tables(length, x.dtype)
    cuts = [int(c) for c in np.cumsum(IN_SIZES)[:-1]]

    for l in range(DEPTH):
        u = rmsnorm(h, norm_mix_g[l]) @ w_in[l]
        z, xbc, dt_raw, q_sb, k_sb, v_sb, q_a, c_kv, k_r = jnp.split(u, cuts, axis=-1)

        y_ssd = ssd_mixer(z, xbc, dt_raw, ssd_conv_w[l], ssd_conv_b[l], ssd_dt_bias[l],
                          ssd_a_log[l], ssd_d[l], ssd_norm_g[l])
        heads = lambda t: t.reshape(bsz, length, SB_HEADS, SB_HEAD_DIM)
        y_sb = rmsnorm(stick_breaking_attention(heads(q_sb), heads(k_sb), heads(v_sb)), sb_norm_g[l])
        y_mla = rmsnorm(mla_attention(q_a, c_kv, k_r, mla_q_norm_g[l], mla_kv_norm_g[l],
                                      mla_w_uq[l], mla_w_ukv[l], cos, sin), mla_norm_g[l])

        h = h + jnp.concatenate([y_ssd, y_sb, y_mla], axis=-1) @ w_out[l]
        h = h + conv_ffn(rmsnorm(h, norm_ffn_g[l]), ffn_w_up[l], ffn_conv_w[l], ffn_conv_b[l],
                         ffn_w_down[l])

    return rmsnorm(h, final_norm_g)[:, N_META:]
```

```python
import functools
import math

import jax
import jax.numpy as jnp
from jax import lax
from jax.experimental import pallas as pl
from jax.experimental.pallas import tpu as pltpu

F32 = jnp.float32
BF16 = jnp.bfloat16

D_MODEL = 1024
N_META = 16
EPS = 1e-6
SSD_HEADS = 8
SSD_HEAD_DIM = 64
SSD_WIDTH = 512
SSD_STATE = 128
SSD_XBC = 1024
SSD_CHUNK = 128
SB_WIDTH = 256
MLA_HEADS = 4
MLA_NOPE = 64
MLA_ROPE = 32
MLA_Q_RANK = 192
MLA_KV_RANK = 128
ROPE_BASE = 10000.0
D_FF = 2816
FF_CHUNK = 256

LANES = 128
SEQ_ALIGN = 256
VMEM_LIMIT = 56 * 1024 * 1024

U_XBC, U_Z, U_QSB, U_KSB, U_VSB, U_QA, U_CKV, U_KR1, U_KR2, U_END = (
    0, 1024, 1536, 1792, 2048, 2304, 2560, 2688, 2816, 2944)

NT_DIMS = (((1,), (1,)), ((), ()))


def _pick(n, candidates):
    for c in candidates:
        if n % c == 0:
            return c
    raise ValueError(f"no tile in {candidates} divides {n}")


def _softplus(x):
    return jnp.maximum(x, 0.0) + jnp.log1p(jnp.exp(-jnp.abs(x)))


def _silu(x):
    return x * jax.nn.sigmoid(x)


def _rms(x, g, n):
    ms = jnp.sum(x * x, axis=-1, keepdims=True) * (1.0 / n)
    return (x * lax.rsqrt(ms + EPS)) * g


def _params(*sem):
    return pltpu.CompilerParams(dimension_semantics=sem, vmem_limit_bytes=VMEM_LIMIT)


def _in_proj_kernel(x_ref, g_ref, w_ref, u_ref, dt_ref):
    xn = _rms(x_ref[...], g_ref[...], D_MODEL).astype(BF16)
    for n0 in range(0, U_END, 512):
        n1 = min(n0 + 512, U_END)
        u_ref[:, n0:n1] = jnp.dot(xn, w_ref[:, n0:n1], preferred_element_type=F32).astype(BF16)
    dt_ref[...] = jnp.dot(xn, w_ref[:, U_END:U_END + SSD_WIDTH], preferred_element_type=F32)


def _in_proj(h2d, g, w):
    rows = h2d.shape[0]
    tm = _pick(rows, (512, 256))
    return pl.pallas_call(
        _in_proj_kernel,
        grid=(rows // tm,),
        in_specs=[
            pl.BlockSpec((tm, D_MODEL), lambda i: (i, 0)),
            pl.BlockSpec((1, D_MODEL), lambda i: (0, 0)),
            pl.BlockSpec(w.shape, lambda i: (0, 0), pipeline_mode=pl.Buffered(1)),
        ],
        out_specs=[
            pl.BlockSpec((tm, U_END), lambda i: (i, 0)),
            pl.BlockSpec((tm, SSD_WIDTH), lambda i: (i, 0)),
        ],
        out_shape=[
            jax.ShapeDtypeStruct((rows, U_END), BF16),
            jax.ShapeDtypeStruct((rows, SSD_WIDTH), F32),
        ],
        compiler_params=_params("parallel"),
        name="in_proj",
    )(h2d, g, w)


def _cumsum_rows(x):
    n = x.shape[0]
    row = lax.broadcasted_iota(jnp.int32, x.shape, 0)
    s = 1
    while s < n:
        x = x + jnp.where(row >= s, pltpu.roll(x, s, 0), 0.0)
        s *= 2
    return x


def _ssd_kernel(z_ref, xbc_ref, dt_ref, cw_ref, cb_ref, dtb_ref, alog_ref, d_ref, g_ref,
                y_ref, ext_ref, st_ref):
    T = SSD_CHUNK

    @pl.when(pl.program_id(1) == 0)
    def _():
        ext_ref[0:8, :] = jnp.zeros((8, SSD_XBC), F32)
        st_ref[...] = jnp.zeros(st_ref.shape, F32)

    ext_ref[8:8 + T, :] = xbc_ref[0].astype(F32)
    cw = cw_ref[...]
    conv = cb_ref[...] + ext_ref[8:8 + T, :] * cw[3:4]
    for k in range(3):
        conv = conv + ext_ref[5 + k:5 + k + T, :] * cw[k:k + 1]
    ext_ref[0:8, :] = ext_ref[T:T + 8, :]
    xc = _silu(conv)
    xs = xc[:, :SSD_WIDTH]
    bm = xc[:, SSD_WIDTH:SSD_WIDTH + 2 * SSD_STATE]
    cm = xc[:, SSD_WIDTH + 2 * SSD_STATE:]

    dt = _softplus(dt_ref[0] + dtb_ref[...])
    acs = _cumsum_rows(dt * (-jnp.exp(alog_ref[...])))
    acs_last = acs[T - 1:T, :]
    x_dt = xs * dt
    xb = x_dt.astype(BF16)
    xd = (x_dt * jnp.exp(acs_last - acs)).astype(BF16)
    eacs = jnp.exp(acs)
    acs_t = acs.T

    tril = (lax.broadcasted_iota(jnp.int32, (T, T), 0) >= lax.broadcasted_iota(jnp.int32, (T, T), 1))
    low_half = lax.broadcasted_iota(jnp.int32, (T, LANES), 1) < SSD_HEAD_DIM
    y_groups = []
    for g in range(2):
        bg = bm[:, g * SSD_STATE:(g + 1) * SSD_STATE]
        cg = cm[:, g * SSD_STATE:(g + 1) * SSD_STATE].astype(BF16)
        cb = lax.dot_general(cg, bg.astype(BF16), NT_DIMS, preferred_element_type=F32)
        pair_out = []
        for pr in range(2):
            lane0 = g * 256 + pr * LANES
            xp = xb[:, lane0:lane0 + LANES]
            res = []
            for hh in range(2):
                hl = lane0 + hh * SSD_HEAD_DIM
                seg = acs[:, hl:hl + 1] - acs_t[hl:hl + 1, :]
                decay = jnp.where(tril, jnp.exp(seg), 0.0)
                res.append(jnp.dot((cb * decay).astype(BF16), xp, preferred_element_type=F32))
            pair_out.append(jnp.where(low_half, res[0], res[1]))
        y_diag = jnp.concatenate(pair_out, axis=1)
        hg = st_ref[g]
        y_off = jnp.dot(cg, hg.astype(BF16), preferred_element_type=F32) * eacs[:, g * 256:(g + 1) * 256]
        st_ref[g] = hg * jnp.exp(acs_last[:, g * 256:(g + 1) * 256]) + jnp.dot(
            bg.T.astype(BF16), xd[:, g * 256:(g + 1) * 256], preferred_element_type=F32)
        y_groups.append(y_diag + y_off)
    y = jnp.concatenate(y_groups, axis=1) + xs * d_ref[...]
    y = y * _silu(z_ref[0].astype(F32))
    y_ref[0] = _rms(y, g_ref[...], SSD_WIDTH).astype(BF16)


def _ssd(u3, dtx3, cw, cb, dtb, alog, dskip, g):
    bsz, lp, _ = u3.shape
    T = SSD_CHUNK
    vec = lambda n: pl.BlockSpec((1, n), lambda b, c: (0, 0))
    return pl.pallas_call(
        _ssd_kernel,
        grid=(bsz, lp // T),
        in_specs=[
            pl.BlockSpec((1, T, SSD_WIDTH), lambda b, c: (b, c, U_Z // SSD_WIDTH)),
            pl.BlockSpec((1, T, SSD_XBC), lambda b, c: (b, c, U_XBC // SSD_XBC)),
            pl.BlockSpec((1, T, SSD_WIDTH), lambda b, c: (b, c, 0)),
            pl.BlockSpec((4, SSD_XBC), lambda b, c: (0, 0)),
            vec(SSD_XBC), vec(SSD_WIDTH), vec(SSD_WIDTH), vec(SSD_WIDTH), vec(SSD_WIDTH),
        ],
        out_specs=pl.BlockSpec((1, T, SSD_WIDTH), lambda b, c: (b, c, 0)),
        out_shape=jax.ShapeDtypeStruct((bsz, lp, SSD_WIDTH), BF16),
        scratch_shapes=[pltpu.VMEM((T + 8, SSD_XBC), F32), pltpu.VMEM((2, SSD_STATE, 256), F32)],
        compiler_params=_params("parallel", "arbitrary"),
        name="ssd_scan",
    )(u3, u3, dtx3, cw, cb, dtb, alog, dskip, g)


def _sb_kernel(q_ref, k_ref, v_ref, m_ref, o_ref, *, tq):
    qi = pl.program_id(2)
    q = q_ref[0].astype(F32)
    lane = lax.broadcasted_iota(jnp.int32, (tq, LANES), 1)
    row = lax.broadcasted_iota(jnp.int32, (tq, tq), 0)
    col = lax.broadcasted_iota(jnp.int32, (tq, tq), 1)
    valid = col < row
    later = m_ref[...]
    outs = []
    for hh in range(2):
        in_head = (lane >= 64 * hh) & (lane < 64 * (hh + 1))
        qm = jnp.where(in_head, q, 0.0).astype(BF16)

        def block(kb, masked, qm=qm):
            start = pl.multiple_of(kb * tq, tq)
            kblk = k_ref[0, pl.ds(start, tq), :]
            vblk = v_ref[0, pl.ds(start, tq), :]
            s = lax.dot_general(qm, kblk, NT_DIMS, preferred_element_type=F32)
            sp = _softplus(s)
            spm = jnp.where(valid, sp, 0.0) if masked else sp
            after = jnp.dot(spm.astype(BF16), later, preferred_element_type=F32)
            w = jnp.exp(s - sp - after)
            if masked:
                w = jnp.where(valid, w, 0.0)
            pv = jnp.dot(w.astype(BF16), vblk, preferred_element_type=F32)
            return pv, jnp.sum(spm, axis=1, keepdims=True)

        pv0, tot0 = block(qi, True)

        def body(i, carry):
            acc, r = carry
            pv, tot = block(qi - 1 - i, False)
            return acc + jnp.exp(r) * pv, r - tot

        acc, _ = lax.fori_loop(0, qi, body, (pv0, -tot0))
        outs.append(acc)
    o_ref[0] = jnp.where(lane < 64, outs[0], outs[1]).astype(BF16)


def _sb_attention(u3, later):
    bsz, lp, _ = u3.shape
    tq = SEQ_ALIGN
    return pl.pallas_call(
        functools.partial(_sb_kernel, tq=tq),
        grid=(bsz, 2, lp // tq),
        in_specs=[
            pl.BlockSpec((1, tq, LANES), lambda b, p, i: (b, i, U_QSB // LANES + p)),
            pl.BlockSpec((1, lp, LANES), lambda b, p, i: (b, 0, U_KSB // LANES + p)),
            pl.BlockSpec((1, lp, LANES), lambda b, p, i: (b, 0, U_VSB // LANES + p)),
            pl.BlockSpec((tq, tq), lambda b, p, i: (0, 0)),
        ],
        out_specs=pl.BlockSpec((1, tq, LANES), lambda b, p, i: (b, i, p)),
        out_shape=jax.ShapeDtypeStruct((bsz, lp, SB_WIDTH), BF16),
        compiler_params=_params("parallel", "parallel", "arbitrary"),
        name="sb_attention",
    )(u3, u3, u3, later)


def _mla_proj_kernel(qa_ref, ckv_ref, kr1_ref, kr2_ref, cos_ref, sin_ref, gq_ref, gkv_ref,
                     wq_ref, wqr_ref, wkv_ref, qc_ref, kc_ref, vm_ref):
    cos = cos_ref[...]
    sin = sin_ref[...]
    cos4 = jnp.concatenate([cos] * MLA_HEADS, axis=1)
    sin4 = jnp.concatenate([sin] * MLA_HEADS, axis=1)
    qn = _rms(qa_ref[...].astype(F32), gq_ref[...], MLA_Q_RANK).astype(BF16)
    q1 = jnp.dot(qn, wq_ref[...], preferred_element_type=F32)
    q2 = jnp.dot(qn, wqr_ref[...], preferred_element_type=F32)
    scale = (MLA_NOPE + MLA_ROPE) ** -0.5
    qc_ref[...] = ((q1 * cos4 + q2 * sin4) * scale).astype(BF16)
    cn = _rms(ckv_ref[...].astype(F32), gkv_ref[...], MLA_KV_RANK).astype(BF16)
    kv = jnp.dot(cn, wkv_ref[...], preferred_element_type=F32)
    krope = kr1_ref[...].astype(F32) * cos + kr2_ref[...].astype(F32) * sin
    kc_ref[...] = (kv[:, :MLA_HEADS * LANES] + jnp.concatenate([krope] * MLA_HEADS, axis=1)).astype(BF16)
    vm_ref[...] = kv[:, MLA_HEADS * LANES:].astype(BF16)


def _mla_proj(u2d, cosx, sinx, gq, gkv, wq, wqr, wkv, lp):
    rows = u2d.shape[0]
    tm = _pick(lp, (768, 512, 256))
    nper = lp // tm
    const = lambda a: pl.BlockSpec(a.shape, lambda i: (0, 0))
    return pl.pallas_call(
        _mla_proj_kernel,
        grid=(rows // tm,),
        in_specs=[
            pl.BlockSpec((tm, 256), lambda i: (i, U_QA // 256)),
            pl.BlockSpec((tm, LANES), lambda i: (i, U_CKV // LANES)),
            pl.BlockSpec((tm, LANES), lambda i: (i, U_KR1 // LANES)),
            pl.BlockSpec((tm, LANES), lambda i: (i, U_KR2 // LANES)),
            pl.BlockSpec((tm, LANES), lambda i: (i % nper, 0)),
            pl.BlockSpec((tm, LANES), lambda i: (i % nper, 0)),
            const(gq), const(gkv), const(wq), const(wqr), const(wkv),
        ],
        out_specs=[
            pl.BlockSpec((tm, MLA_HEADS * LANES), lambda i: (i, 0)),
            pl.BlockSpec((tm, MLA_HEADS * LANES), lambda i: (i, 0)),
            pl.BlockSpec((tm, 256), lambda i: (i, 0)),
        ],
        out_shape=[
            jax.ShapeDtypeStruct((rows, MLA_HEADS * LANES), BF16),
            jax.ShapeDtypeStruct((rows, MLA_HEADS * LANES), BF16),
            jax.ShapeDtypeStruct((rows, 256), BF16),
        ],
        compiler_params=_params("parallel"),
        name="mla_proj",
    )(u2d, u2d, u2d, u2d, cosx, sinx, gq, gkv, wq, wqr, wkv)


def _mla_kernel(q_ref, k_ref, v_ref, o_ref, *, tq):
    qi = pl.program_id(2)
    lane = lax.broadcasted_iota(jnp.int32, (tq, LANES), 1)
    row = lax.broadcasted_iota(jnp.int32, (tq, tq), 0)
    col = lax.broadcasted_iota(jnp.int32, (tq, tq), 1)
    causal = col <= row
    outs = []
    for hh in range(2):
        qh = q_ref[0, :, hh * LANES:(hh + 1) * LANES]

        def scores(kb, hh=hh, qh=qh):
            start = pl.multiple_of(kb * tq, tq)
            kblk = k_ref[0, pl.ds(start, tq), hh * LANES:(hh + 1) * LANES]
            vblk = v_ref[0, pl.ds(start, tq), :]
            return lax.dot_general(qh, kblk, NT_DIMS, preferred_element_type=F32), vblk

        s, vblk = scores(qi)
        s = jnp.where(causal, s, -1e30)
        m0 = jnp.max(s, axis=1, keepdims=True)
        p = jnp.exp(s - m0)
        l0 = jnp.sum(p, axis=1, keepdims=True)
        acc0 = jnp.dot(p.astype(BF16), vblk, preferred_element_type=F32)

        def body(i, carry):
            m, l, acc = carry
            s, vblk = scores(qi - 1 - i)
            m_new = jnp.maximum(m, jnp.max(s, axis=1, keepdims=True))
            alpha = jnp.exp(m - m_new)
            p = jnp.exp(s - m_new)
            l = alpha * l + jnp.sum(p, axis=1, keepdims=True)
            acc = alpha * acc + jnp.dot(p.astype(BF16), vblk, preferred_element_type=F32)
            return m_new, l, acc

        _, l, acc = lax.fori_loop(0, qi, body, (m0, l0, acc0))
        outs.append(acc / l)
    o_ref[0] = jnp.where(lane < 64, outs[0], outs[1]).astype(BF16)


def _mla_attention(qc3, kc3, vm3):
    bsz, lp, _ = qc3.shape
    tq = SEQ_ALIGN
    return pl.pallas_call(
        functools.partial(_mla_kernel, tq=tq),
        grid=(bsz, 2, lp // tq),
        in_specs=[
            pl.BlockSpec((1, tq, 2 * LANES), lambda b, p, i: (b, i, p)),
            pl.BlockSpec((1, lp, 2 * LANES), lambda b, p, i: (b, 0, p)),
            pl.BlockSpec((1, lp, LANES), lambda b, p, i: (b, 0, p)),
        ],
        out_specs=pl.BlockSpec((1, tq, LANES), lambda b, p, i: (b, i, p)),
        out_shape=jax.ShapeDtypeStruct((bsz, lp, 256), BF16),
        compiler_params=_params("parallel", "parallel", "arbitrary"),
        name="mla_attention",
    )(qc3, kc3, vm3)


def _out_ffn_kernel(h_ref, yssd_ref, osb_ref, omla_ref, gsb_ref, gmla_ref, wout_ref, gffn_ref,
                    wup_ref, cw_ref, cb_ref, wdown_ref, o_ref, halo_ref, ext_ref, act_ref, *, tm):
    @pl.when(pl.program_id(1) == 0)
    def _():
        halo_ref[...] = jnp.zeros(halo_ref.shape, F32)

    ysb = _rms(osb_ref[0].astype(F32), gsb_ref[...], SB_WIDTH).astype(BF16)
    ymla = _rms(omla_ref[0].astype(F32), gmla_ref[...], 256).astype(BF16)
    mix = jnp.dot(yssd_ref[0], wout_ref[0:512, :], preferred_element_type=F32)
    mix = mix + jnp.dot(ysb, wout_ref[512:768, :], preferred_element_type=F32)
    mix = mix + jnp.dot(ymla, wout_ref[768:1024, :], preferred_element_type=F32)
    h1 = h_ref[0] + mix
    o_ref[0] = h1
    xn = _rms(h1, gffn_ref[...], D_MODEL).astype(BF16)

    w2 = 2 * FF_CHUNK
    for c in range(D_FF // FF_CHUNK):
        cols = slice(c * w2, (c + 1) * w2)
        ext_ref[0:8, :] = halo_ref[:, cols]
        ext_ref[8:8 + tm, :] = jnp.dot(xn, wup_ref[:, cols], preferred_element_type=F32)
        halo_ref[:, cols] = ext_ref[tm:tm + 8, :]
        cw = cw_ref[:, cols]
        conv = cb_ref[:, cols] + ext_ref[8:8 + tm, :] * cw[2:3]
        conv = conv + ext_ref[7:7 + tm, :] * cw[1:2]
        conv = conv + ext_ref[6:6 + tm, :] * cw[0:1]
        act = _silu(conv[:, :FF_CHUNK]) * conv[:, FF_CHUNK:]
        act_ref[:, c * FF_CHUNK:(c + 1) * FF_CHUNK] = act.astype(BF16)
    o_ref[0] = o_ref[0] + jnp.dot(act_ref[...], wdown_ref[...], preferred_element_type=F32)


def _out_ffn(h3, yssd, osb, omla, gsb, gmla, wout, gffn, wup, cw, cb, wdown):
    bsz, lp, _ = h3.shape
    tm = _pick(lp, (768, 256))
    const = lambda a: pl.BlockSpec(a.shape, lambda b, t: (0, 0), pipeline_mode=pl.Buffered(1))
    rowblk = lambda n: pl.BlockSpec((1, tm, n), lambda b, t: (b, t, 0))
    return pl.pallas_call(
        functools.partial(_out_ffn_kernel, tm=tm),
        grid=(bsz, lp // tm),
        in_specs=[rowblk(D_MODEL), rowblk(SSD_WIDTH), rowblk(256), rowblk(256),
                  const(gsb), const(gmla), const(wout), const(gffn),
                  const(wup), const(cw), const(cb), const(wdown)],
        out_specs=rowblk(D_MODEL),
        out_shape=jax.ShapeDtypeStruct(h3.shape, F32),
        scratch_shapes=[pltpu.VMEM((8, 2 * D_FF), F32),
                        pltpu.VMEM((tm + 8, 2 * FF_CHUNK), F32),
                        pltpu.VMEM((tm, D_FF), BF16)],
        compiler_params=_params("parallel", "arbitrary"),
        name="out_ffn",
    )(h3, yssd, osb, omla, gsb, gmla, wout, gffn, wup, cw, cb, wdown)


def _final_kernel(a_ref, t_ref, g_ref, o_ref, *, tm):
    g = g_ref[...]
    o_ref[0, 0:tm - N_META, :] = _rms(a_ref[0, N_META:tm, :], g, D_MODEL)
    o_ref[0, tm - N_META:tm, :] = _rms(t_ref[0], g, D_MODEL)


def _final_norm(h3, g, seq):
    bsz = h3.shape[0]
    tm = _pick(seq, (512, 128))
    return pl.pallas_call(
        functools.partial(_final_kernel, tm=tm),
        grid=(bsz, seq // tm),
        in_specs=[
            pl.BlockSpec((1, tm, D_MODEL), lambda b, i: (b, i, 0)),
            pl.BlockSpec((1, N_META, D_MODEL), lambda b, i: (b, (i + 1) * (tm // N_META), 0)),
            pl.BlockSpec((1, D_MODEL), lambda b, i: (0, 0)),
        ],
        out_specs=pl.BlockSpec((1, tm, D_MODEL), lambda b, i: (b, i, 0)),
        out_shape=jax.ShapeDtypeStruct((bsz, seq, D_MODEL), F32),
        compiler_params=_params("parallel", "parallel"),
        name="final_norm",
    )(h3, h3, g)


def _rot_cols(w):
    half = w.shape[1] // 2
    return jnp.concatenate([-w[:, half:], w[:, :half]], axis=1)


def _prep_in_proj(w_in):
    zeros = lambda n: jnp.zeros((D_MODEL, n), F32)
    cuts = (512, 1536, 1544, 1800, 2056, 2312, 2504, 2632)
    z, xbc, dt, q_sb, k_sb, v_sb, q_a, c_kv, k_r = jnp.split(w_in, cuts, axis=1)
    kr1 = jnp.concatenate([zeros(MLA_NOPE), k_r, zeros(LANES - MLA_NOPE - MLA_ROPE)], axis=1)
    kr2 = jnp.concatenate([zeros(MLA_NOPE), _rot_cols(k_r), zeros(LANES - MLA_NOPE - MLA_ROPE)], axis=1)
    w = jnp.concatenate([xbc, z, q_sb * (64 ** -0.5), k_sb, v_sb, q_a, zeros(256 - MLA_Q_RANK), c_kv,
                         kr1, kr2, jnp.repeat(dt, SSD_HEAD_DIM, axis=1)], axis=1)
    return w.astype(BF16)


def _prep_mla(w_uq, w_ukv):
    qh = w_uq.reshape(MLA_Q_RANK, MLA_HEADS, MLA_NOPE + MLA_ROPE)
    pad = jnp.zeros((MLA_Q_RANK, MLA_HEADS, LANES - MLA_NOPE - MLA_ROPE), F32)
    nope0 = jnp.zeros((MLA_Q_RANK, MLA_HEADS, MLA_NOPE), F32)
    rope = qh[..., MLA_NOPE:]
    rope_rot = jnp.concatenate([-rope[..., MLA_ROPE // 2:], rope[..., :MLA_ROPE // 2]], axis=-1)
    wq = jnp.concatenate([qh, pad], axis=-1).reshape(MLA_Q_RANK, MLA_HEADS * LANES)
    wqr = jnp.concatenate([nope0, rope_rot, pad], axis=-1).reshape(MLA_Q_RANK, MLA_HEADS * LANES)
    rowpad = jnp.zeros((256 - MLA_Q_RANK, MLA_HEADS * LANES), F32)
    wq = jnp.concatenate([wq, rowpad], axis=0).astype(BF16)
    wqr = jnp.concatenate([wqr, rowpad], axis=0).astype(BF16)
    kvh = w_ukv.reshape(MLA_KV_RANK, MLA_HEADS, MLA_NOPE + 64)
    kpad = jnp.zeros((MLA_KV_RANK, MLA_HEADS, LANES - MLA_NOPE), F32)
    wk = jnp.concatenate([kvh[..., :MLA_NOPE], kpad], axis=-1).reshape(MLA_KV_RANK, MLA_HEADS * LANES)
    wv = kvh[..., MLA_NOPE:].reshape(MLA_KV_RANK, MLA_HEADS * 64)
    return wq, wqr, jnp.concatenate([wk, wv], axis=1).astype(BF16)


def _interleave_ff(a):
    lead = a.shape[:-1]
    g = a[..., :D_FF].reshape(lead + (D_FF // FF_CHUNK, FF_CHUNK))
    v = a[..., D_FF:].reshape(lead + (D_FF // FF_CHUNK, FF_CHUNK))
    return jnp.concatenate([g, v], axis=-1).reshape(lead + (2 * D_FF,))


def _rope_tables(lp):
    pos = jnp.arange(lp, dtype=F32)
    inv = 1.0 / (ROPE_BASE ** (jnp.arange(0, MLA_ROPE, 2, dtype=F32) / MLA_ROPE))
    ang = pos[:, None] * inv[None, :]
    ang = jnp.concatenate([ang, ang], axis=-1)
    ones = jnp.ones((lp, MLA_NOPE), F32)
    tail = LANES - MLA_NOPE - MLA_ROPE
    cosx = jnp.concatenate([ones, jnp.cos(ang), jnp.ones((lp, tail), F32)], axis=1)
    sinx = jnp.concatenate([0 * ones, jnp.sin(ang), jnp.zeros((lp, tail), F32)], axis=1)
    return cosx, sinx


def kernel(x, meta_tokens, norm_mix_g, w_in, ssd_conv_w, ssd_conv_b, ssd_dt_bias, ssd_a_log, ssd_d, ssd_norm_g, sb_norm_g, mla_q_norm_g, mla_kv_norm_g, mla_w_uq, mla_w_ukv, mla_norm_g, w_out, norm_ffn_g, ffn_w_up, ffn_conv_w, ffn_conv_b, ffn_w_down, final_norm_g):
    bsz, seq, _ = x.shape
    length = N_META + seq
    lp = -(-length // SEQ_ALIGN) * SEQ_ALIGN
    depth = w_in.shape[0]

    meta = jnp.broadcast_to(meta_tokens[None].astype(x.dtype), (bsz, N_META, D_MODEL))
    h = jnp.concatenate([meta, x, jnp.zeros((bsz, lp - length, D_MODEL), x.dtype)], axis=1)

    cosx, sinx = _rope_tables(lp)
    tq = SEQ_ALIGN
    later = (lax.broadcasted_iota(jnp.int32, (tq, tq), 0) >
             lax.broadcasted_iota(jnp.int32, (tq, tq), 1)).astype(BF16)
    row = lambda v: v.reshape(1, -1)
    rep = lambda v: jnp.repeat(v, SSD_HEAD_DIM).reshape(1, -1)

    for l in range(depth):
        u, dtx = _in_proj(h.reshape(bsz * lp, D_MODEL), row(norm_mix_g[l]), _prep_in_proj(w_in[l]))
        u3 = u.reshape(bsz, lp, U_END)
        y_ssd = _ssd(u3, dtx.reshape(bsz, lp, SSD_WIDTH), ssd_conv_w[l], row(ssd_conv_b[l]),
                     rep(ssd_dt_bias[l]), rep(ssd_a_log[l]), rep(ssd_d[l]), row(ssd_norm_g[l]))
        o_sb = _sb_attention(u3, later)
        wq, wqr, wkv = _prep_mla(mla_w_uq[l], mla_w_ukv[l])
        gq = jnp.concatenate([mla_q_norm_g[l], jnp.zeros((256 - MLA_Q_RANK,), F32)]).reshape(1, -1)
        qc, kc, vm = _mla_proj(u, cosx, sinx, gq, row(mla_kv_norm_g[l]), wq, wqr, wkv, lp)
        o_mla = _mla_attention(qc.reshape(bsz, lp, -1), kc.reshape(bsz, lp, -1), vm.reshape(bsz, lp, -1))
        h = _out_ffn(h, y_ssd, o_sb, o_mla, row(sb_norm_g[l]), row(mla_norm_g[l]),
                     w_out[l].astype(BF16), row(norm_ffn_g[l]),
                     _interleave_ff(ffn_w_up[l]).astype(BF16), _interleave_ff(ffn_conv_w[l]),
                     row(_interleave_ff(ffn_conv_b[l])), ffn_w_down[l].astype(BF16))

    return _final_norm(h, row(final_norm_g), seq)
```

```python
import functools
import math

import jax
import jax.numpy as jnp
from jax import lax
from jax.experimental import pallas as pl
from jax.experimental.pallas import tpu as pltpu

F32 = jnp.float32
BF16 = jnp.bfloat16

D_MODEL = 1024
N_META = 16
EPS = 1e-6
SSD_HEADS = 8
SSD_HEAD_DIM = 64
SSD_WIDTH = 512
SSD_STATE = 128
SSD_XBC = 1024
SSD_CHUNK = 128
SB_WIDTH = 256
MLA_HEADS = 4
MLA_NOPE = 64
MLA_ROPE = 32
MLA_Q_RANK = 192
MLA_KV_RANK = 128
ROPE_BASE = 10000.0
D_FF = 2816
FF_CHUNK = 256

LANES = 128
SEQ_ALIGN = 256
ATT_SUB = 256
ATT_CHUNK = 1024
VMEM_LIMIT = 56 * 1024 * 1024

U_XBC, U_Z, U_QSB, U_KSB, U_VSB, U_QA, U_CKV, U_KR1, U_KR2, U_END = (
    0, 1024, 1536, 1792, 2048, 2304, 2560, 2688, 2816, 2944)

NT_DIMS = (((1,), (1,)), ((), ()))
LOG2E = 1.4426950408889634


def _pick(n, candidates):
    for c in candidates:
        if n % c == 0:
            return c
    raise ValueError(f"no tile in {candidates} divides {n}")


def _softplus(x):
    return jnp.maximum(x, 0.0) + jnp.log(1.0 + jnp.exp2(jnp.abs(x) * (-LOG2E)))


def _silu(x):
    return x * jax.nn.sigmoid(x)


def _rms(x, g, n):
    ms = jnp.sum(x * x, axis=-1, keepdims=True) * (1.0 / n)
    return (x * lax.rsqrt(ms + EPS)) * g


def _params(*sem):
    return pltpu.CompilerParams(dimension_semantics=sem, vmem_limit_bytes=VMEM_LIMIT)


def _in_proj_kernel(x_ref, g_ref, w_ref, u_ref, dt_ref):
    xn = _rms(x_ref[...], g_ref[...], D_MODEL).astype(BF16)
    for n0 in range(0, U_END, 512):
        n1 = min(n0 + 512, U_END)
        u_ref[:, n0:n1] = jnp.dot(xn, w_ref[:, n0:n1], preferred_element_type=F32).astype(BF16)
    dt_ref[...] = jnp.dot(xn, w_ref[:, U_END:U_END + SSD_WIDTH], preferred_element_type=F32)


def _in_proj(h2d, g, w):
    rows = h2d.shape[0]
    tm = _pick(rows, (512, 256))
    return pl.pallas_call(
        _in_proj_kernel,
        grid=(rows // tm,),
        in_specs=[
            pl.BlockSpec((tm, D_MODEL), lambda i: (i, 0)),
            pl.BlockSpec((1, D_MODEL), lambda i: (0, 0)),
            pl.BlockSpec(w.shape, lambda i: (0, 0), pipeline_mode=pl.Buffered(1)),
        ],
        out_specs=[
            pl.BlockSpec((tm, U_END), lambda i: (i, 0)),
            pl.BlockSpec((tm, SSD_WIDTH), lambda i: (i, 0)),
        ],
        out_shape=[
            jax.ShapeDtypeStruct((rows, U_END), BF16),
            jax.ShapeDtypeStruct((rows, SSD_WIDTH), F32),
        ],
        compiler_params=_params("parallel"),
        name="in_proj",
    )(h2d, g, w)


def _cumsum_rows(x):
    n = x.shape[0]
    row = lax.broadcasted_iota(jnp.int32, x.shape, 0)
    s = 1
    while s < n:
        x = x + jnp.where(row >= s, pltpu.roll(x, s, 0), 0.0)
        s *= 2
    return x


def _ssd_kernel(z_ref, xbc_ref, dt_ref, cw_ref, cb_ref, dtb_ref, alog_ref, d_ref, g_ref,
                y_ref, ext_ref, st_ref):
    T = SSD_CHUNK

    @pl.when(pl.program_id(1) == 0)
    def _():
        ext_ref[0:8, :] = jnp.zeros((8, SSD_XBC), F32)
        st_ref[...] = jnp.zeros(st_ref.shape, F32)

    ext_ref[8:8 + T, :] = xbc_ref[0].astype(F32)
    cw = cw_ref[...]
    conv = cb_ref[...] + ext_ref[8:8 + T, :] * cw[3:4]
    for k in range(3):
        conv = conv + ext_ref[5 + k:5 + k + T, :] * cw[k:k + 1]
    ext_ref[0:8, :] = ext_ref[T:T + 8, :]
    xc = _silu(conv)
    xs = xc[:, :SSD_WIDTH]
    bm = xc[:, SSD_WIDTH:SSD_WIDTH + 2 * SSD_STATE]
    cm = xc[:, SSD_WIDTH + 2 * SSD_STATE:]

    dt = _softplus(dt_ref[0] + dtb_ref[...])
    acs = _cumsum_rows(dt * (-jnp.exp(alog_ref[...])))
    acs_last = acs[T - 1:T, :]
    x_dt = xs * dt
    xb = x_dt.astype(BF16)
    xd = (x_dt * jnp.exp(acs_last - acs)).astype(BF16)
    eacs = jnp.exp(acs)
    acs_t = acs.T

    tril = (lax.broadcasted_iota(jnp.int32, (T, T), 0) >= lax.broadcasted_iota(jnp.int32, (T, T), 1))
    low_half = lax.broadcasted_iota(jnp.int32, (T, LANES), 1) < SSD_HEAD_DIM
    y_groups = []
    for g in range(2):
        bg = bm[:, g * SSD_STATE:(g + 1) * SSD_STATE]
        cg = cm[:, g * SSD_STATE:(g + 1) * SSD_STATE].astype(BF16)
        cb = lax.dot_general(cg, bg.astype(BF16), NT_DIMS, preferred_element_type=F32)
        pair_out = []
        for pr in range(2):
            lane0 = g * 256 + pr * LANES
            xp = xb[:, lane0:lane0 + LANES]
            res = []
            for hh in range(2):
                hl = lane0 + hh * SSD_HEAD_DIM
                seg = acs[:, hl:hl + 1] - acs_t[hl:hl + 1, :]
                decay = jnp.where(tril, jnp.exp(seg), 0.0)
                res.append(jnp.dot((cb * decay).astype(BF16), xp, preferred_element_type=F32))
            pair_out.append(jnp.where(low_half, res[0], res[1]))
        y_diag = jnp.concatenate(pair_out, axis=1)
        hg = st_ref[g]
        y_off = jnp.dot(cg, hg.astype(BF16), preferred_element_type=F32) * eacs[:, g * 256:(g + 1) * 256]
        st_ref[g] = hg * jnp.exp(acs_last[:, g * 256:(g + 1) * 256]) + jnp.dot(
            bg.T.astype(BF16), xd[:, g * 256:(g + 1) * 256], preferred_element_type=F32)
        y_groups.append(y_diag + y_off)
    y = jnp.concatenate(y_groups, axis=1) + xs * d_ref[...]
    y = y * _silu(z_ref[0].astype(F32))
    y_ref[0] = _rms(y, g_ref[...], SSD_WIDTH).astype(BF16)


def _ssd(u3, dtx3, cw, cb, dtb, alog, dskip, g):
    bsz, lp, _ = u3.shape
    T = SSD_CHUNK
    vec = lambda n: pl.BlockSpec((1, n), lambda b, c: (0, 0))
    return pl.pallas_call(
        _ssd_kernel,
        grid=(bsz, lp // T),
        in_specs=[
            pl.BlockSpec((1, T, SSD_WIDTH), lambda b, c: (b, c, U_Z // SSD_WIDTH)),
            pl.BlockSpec((1, T, SSD_XBC), lambda b, c: (b, c, U_XBC // SSD_XBC)),
            pl.BlockSpec((1, T, SSD_WIDTH), lambda b, c: (b, c, 0)),
            pl.BlockSpec((4, SSD_XBC), lambda b, c: (0, 0)),
            vec(SSD_XBC), vec(SSD_WIDTH), vec(SSD_WIDTH), vec(SSD_WIDTH), vec(SSD_WIDTH),
        ],
        out_specs=pl.BlockSpec((1, T, SSD_WIDTH), lambda b, c: (b, c, 0)),
        out_shape=jax.ShapeDtypeStruct((bsz, lp, SSD_WIDTH), BF16),
        scratch_shapes=[pltpu.VMEM((T + 8, SSD_XBC), F32), pltpu.VMEM((2, SSD_STATE, 256), F32)],
        compiler_params=_params("parallel", "arbitrary"),
        name="ssd_scan",
    )(u3, u3, dtx3, cw, cb, dtb, alog, dskip, g)


def _sub_start(start, j):
    ks = start + j * ATT_SUB
    return ks if isinstance(ks, int) else pl.multiple_of(ks, ATT_SUB)


def _chunk_plan(qi, tq, tk, lp):
    n_full = (qi * tq) // tk
    lo = n_full * tk
    return n_full, lo, jnp.minimum(lo, lp - tk)


def _sb_kernel(q_ref, k_ref, vt_ref, lt_ref, o_ref, s_scr, acc_scr, r_scr, *, tq, tk, lp):
    qi = pl.program_id(2)
    nsub = tk // ATT_SUB
    n_full, lo, start_m = _chunk_plan(qi, tq, tk, lp)
    q = q_ref[0].astype(F32)
    lane = lax.broadcasted_iota(jnp.int32, (tq, LANES), 1)
    qms = [jnp.where((lane >= 64 * hh) & (lane < 64 * (hh + 1)), q, 0.0).astype(BF16) for hh in range(2)]
    later_t = lt_ref[...]
    sub_iota = lax.broadcasted_iota(jnp.int32, (ATT_SUB, tq), 0)
    qpos = qi * tq + lax.broadcasted_iota(jnp.int32, (ATT_SUB, tq), 1)
    sub = lambda j: slice(j * ATT_SUB, (j + 1) * ATT_SUB)

    def scores(start, hh):
        return [lax.dot_general(k_ref[0, pl.ds(_sub_start(start, j), ATT_SUB), :],
                                qms[hh], NT_DIMS, preferred_element_type=F32) for j in range(nsub)]

    def load_scores(hh):
        return [s_scr[hh, sub(j), :] for j in range(nsub)]

    def store_scores(hh, subs):
        for j in range(nsub):
            s_scr[hh, sub(j), :] = subs[j]

    def update(subs, start, hh, first):
        carry = None
        pv = None
        for j in reversed(range(nsub)):
            ks = _sub_start(start, j)
            s = subs[j]
            sp = _softplus(s)
            if first:
                key = ks + sub_iota
                valid = (key >= lo) & (key < qpos)
                sp = jnp.where(valid, sp, 0.0)
            spb = sp.astype(BF16)
            loc = jnp.dot(later_t, spb, preferred_element_type=F32)
            w = jnp.exp(s - sp - loc)
            if first:
                w = jnp.where(valid, w, 0.0)
            d = jnp.dot(vt_ref[0, :, pl.ds(ks, ATT_SUB)], w.astype(BF16), preferred_element_type=F32)
            tot = loc[0:1, :] + spb[0:1, :].astype(F32)
            if carry is None:
                pv, carry = d, tot
            else:
                pv, carry = pv + jnp.exp(-carry) * d, carry + tot
        if first:
            acc_scr[hh] = pv
            r_scr[hh] = -carry
        else:
            r = r_scr[hh]
            acc_scr[hh] = acc_scr[hh] + jnp.exp(r) * pv
            r_scr[hh] = r - carry

    for hh in range(2):
        update(scores(start_m, hh), start_m, hh, True)
        store_scores(hh, scores(jnp.maximum(n_full - 1, 0) * tk, hh))

    def body(i, c):
        start = (n_full - 1 - i) * tk
        for hh in range(2):
            cur = load_scores(hh)
            store_scores(hh, scores(start - tk, hh))
            update(cur, start, hh, False)
        return c

    lax.fori_loop(0, n_full - 1, body, 0)

    @pl.when(n_full > 0)
    def _():
        for hh in range(2):
            update(load_scores(hh), 0, hh, False)

    head_row = lax.broadcasted_iota(jnp.int32, (LANES, tq), 0)
    o_ref[0] = jnp.where(head_row < 64, acc_scr[0], acc_scr[1]).T.astype(BF16)


def _sb_attention(u3, vt, later_t):
    bsz, lp, _ = u3.shape
    tq = SEQ_ALIGN
    tk = min(ATT_CHUNK, lp)
    return pl.pallas_call(
        functools.partial(_sb_kernel, tq=tq, tk=tk, lp=lp),
        grid=(bsz, 2, lp // tq),
        in_specs=[
            pl.BlockSpec((1, tq, LANES), lambda b, p, i: (b, i, U_QSB // LANES + p)),
            pl.BlockSpec((1, lp, LANES), lambda b, p, i: (b, 0, U_KSB // LANES + p)),
            pl.BlockSpec((1, LANES, lp), lambda b, p, i: (b, p, 0)),
            pl.BlockSpec((ATT_SUB, ATT_SUB), lambda b, p, i: (0, 0)),
        ],
        out_specs=pl.BlockSpec((1, tq, LANES), lambda b, p, i: (b, i, p)),
        out_shape=jax.ShapeDtypeStruct((bsz, lp, SB_WIDTH), BF16),
        scratch_shapes=[pltpu.VMEM((2, tk, tq), F32), pltpu.VMEM((2, LANES, tq), F32),
                        pltpu.VMEM((2, 1, tq), F32)],
        compiler_params=_params("parallel", "parallel", "arbitrary"),
        name="sb_attention",
    )(u3, u3, vt, later_t)


def _mla_proj_kernel(qa_ref, ckv_ref, kr1_ref, kr2_ref, cos_ref, sin_ref, gq_ref, gkv_ref,
                     wq_ref, wqr_ref, wk_ref, wvt_ref, qc_ref, kc_ref, vt_ref):
    cos = cos_ref[...]
    sin = sin_ref[...]
    cos4 = jnp.concatenate([cos] * MLA_HEADS, axis=1)
    sin4 = jnp.concatenate([sin] * MLA_HEADS, axis=1)
    qn = _rms(qa_ref[...].astype(F32), gq_ref[...], MLA_Q_RANK).astype(BF16)
    q1 = jnp.dot(qn, wq_ref[...], preferred_element_type=F32)
    q2 = jnp.dot(qn, wqr_ref[...], preferred_element_type=F32)
    scale = (MLA_NOPE + MLA_ROPE) ** -0.5
    qc_ref[...] = ((q1 * cos4 + q2 * sin4) * scale).astype(BF16)
    cn = _rms(ckv_ref[...].astype(F32), gkv_ref[...], MLA_KV_RANK).astype(BF16)
    kn = jnp.dot(cn, wk_ref[...], preferred_element_type=F32)
    krope = kr1_ref[...].astype(F32) * cos + kr2_ref[...].astype(F32) * sin
    kc_ref[...] = (kn + jnp.concatenate([krope] * MLA_HEADS, axis=1)).astype(BF16)
    vt_ref[0] = lax.dot_general(wvt_ref[...], cn, NT_DIMS, preferred_element_type=F32).astype(BF16)


def _mla_proj(u2d, cosx, sinx, gq, gkv, wq, wqr, wk, wvt, bsz, lp):
    rows = u2d.shape[0]
    tm = _pick(lp, (768, 512, 256))
    nper = lp // tm
    const = lambda a: pl.BlockSpec(a.shape, lambda i: (0, 0))
    return pl.pallas_call(
        _mla_proj_kernel,
        grid=(rows // tm,),
        in_specs=[
            pl.BlockSpec((tm, 256), lambda i: (i, U_QA // 256)),
            pl.BlockSpec((tm, LANES), lambda i: (i, U_CKV // LANES)),
            pl.BlockSpec((tm, LANES), lambda i: (i, U_KR1 // LANES)),
            pl.BlockSpec((tm, LANES), lambda i: (i, U_KR2 // LANES)),
            pl.BlockSpec((tm, LANES), lambda i: (i % nper, 0)),
            pl.BlockSpec((tm, LANES), lambda i: (i % nper, 0)),
            const(gq), const(gkv), const(wq), const(wqr), const(wk), const(wvt),
        ],
        out_specs=[
            pl.BlockSpec((tm, MLA_HEADS * LANES), lambda i: (i, 0)),
            pl.BlockSpec((tm, MLA_HEADS * LANES), lambda i: (i, 0)),
            pl.BlockSpec((1, 256, tm), lambda i: (i // nper, 0, i % nper)),
        ],
        out_shape=[
            jax.ShapeDtypeStruct((rows, MLA_HEADS * LANES), BF16),
            jax.ShapeDtypeStruct((rows, MLA_HEADS * LANES), BF16),
            jax.ShapeDtypeStruct((bsz, 256, lp), BF16),
        ],
        compiler_params=_params("parallel"),
        name="mla_proj",
    )(u2d, u2d, u2d, u2d, cosx, sinx, gq, gkv, wq, wqr, wk, wvt)


def _mla_kernel(q_ref, k_ref, vt_ref, o_ref, s_scr, acc_scr, m_scr, l_scr, *, tq, tk, lp):
    qi = pl.program_id(2)
    nsub = tk // ATT_SUB
    n_full, lo, start_m = _chunk_plan(qi, tq, tk, lp)
    qs = [q_ref[0, :, hh * LANES:(hh + 1) * LANES] for hh in range(2)]
    sub_iota = lax.broadcasted_iota(jnp.int32, (ATT_SUB, tq), 0)
    qpos = qi * tq + lax.broadcasted_iota(jnp.int32, (ATT_SUB, tq), 1)
    sub = lambda j: slice(j * ATT_SUB, (j + 1) * ATT_SUB)

    def scores(start, hh):
        return [lax.dot_general(k_ref[0, pl.ds(_sub_start(start, j), ATT_SUB), hh * LANES:(hh + 1) * LANES],
                                qs[hh], NT_DIMS, preferred_element_type=F32) for j in range(nsub)]

    def load_scores(hh):
        return [s_scr[hh, sub(j), :] for j in range(nsub)]

    def store_scores(hh, subs):
        for j in range(nsub):
            s_scr[hh, sub(j), :] = subs[j]

    def update(subs, start, hh, first):
        if first:
            masked = []
            for j in range(nsub):
                key = _sub_start(start, j) + sub_iota
                masked.append(jnp.where((key >= lo) & (key <= qpos), subs[j], -1e30))
            subs = masked
        cmax = functools.reduce(jnp.maximum, [jnp.max(s, axis=0, keepdims=True) for s in subs])
        m_new = cmax if first else jnp.maximum(m_scr[hh], cmax)
        psum = None
        pv = None
        for j in range(nsub):
            p = jnp.exp(subs[j] - m_new)
            d = jnp.dot(vt_ref[0, :, pl.ds(_sub_start(start, j), ATT_SUB)], p.astype(BF16),
                        preferred_element_type=F32)
            ps = jnp.sum(p, axis=0, keepdims=True)
            psum = ps if psum is None else psum + ps
            pv = d if pv is None else pv + d
        if first:
            l_scr[hh] = psum
            acc_scr[hh] = pv
        else:
            alpha = jnp.exp(m_scr[hh] - m_new)
            l_scr[hh] = alpha * l_scr[hh] + psum
            acc_scr[hh] = alpha * acc_scr[hh] + pv
        m_scr[hh] = m_new

    for hh in range(2):
        update(scores(start_m, hh), start_m, hh, True)
        store_scores(hh, scores(jnp.maximum(n_full - 1, 0) * tk, hh))

    def body(i, c):
        start = (n_full - 1 - i) * tk
        for hh in range(2):
            cur = load_scores(hh)
            store_scores(hh, scores(start - tk, hh))
            update(cur, start, hh, False)
        return c

    lax.fori_loop(0, n_full - 1, body, 0)

    @pl.when(n_full > 0)
    def _():
        for hh in range(2):
            update(load_scores(hh), 0, hh, False)

    head_row = lax.broadcasted_iota(jnp.int32, (LANES, tq), 0)
    o_ref[0] = jnp.where(head_row < 64, acc_scr[0] / l_scr[0], acc_scr[1] / l_scr[1]).T.astype(BF16)


def _mla_attention(qc3, kc3, vt):
    bsz, lp, _ = qc3.shape
    tq = SEQ_ALIGN
    tk = min(ATT_CHUNK, lp)
    return pl.pallas_call(
        functools.partial(_mla_kernel, tq=tq, tk=tk, lp=lp),
        grid=(bsz, 2, lp // tq),
        in_specs=[
            pl.BlockSpec((1, tq, 2 * LANES), lambda b, p, i: (b, i, p)),
            pl.BlockSpec((1, lp, 2 * LANES), lambda b, p, i: (b, 0, p)),
            pl.BlockSpec((1, LANES, lp), lambda b, p, i: (b, p, 0)),
        ],
        out_specs=pl.BlockSpec((1, tq, LANES), lambda b, p, i: (b, i, p)),
        out_shape=jax.ShapeDtypeStruct((bsz, lp, 256), BF16),
        scratch_shapes=[pltpu.VMEM((2, tk, tq), F32), pltpu.VMEM((2, LANES, tq), F32),
                        pltpu.VMEM((2, 1, tq), F32), pltpu.VMEM((2, 1, tq), F32)],
        compiler_params=_params("parallel", "parallel", "arbitrary"),
        name="mla_attention",
    )(qc3, kc3, vt)


def _out_ffn_kernel(h_ref, yssd_ref, osb_ref, omla_ref, gsb_ref, gmla_ref, wout_ref, gffn_ref,
                    wup_ref, cw_ref, cb_ref, wdown_ref, o_ref, halo_ref, ext_ref, act_ref, *, tm):
    @pl.when(pl.program_id(1) == 0)
    def _():
        halo_ref[...] = jnp.zeros(halo_ref.shape, F32)

    ysb = _rms(osb_ref[0].astype(F32), gsb_ref[...], SB_WIDTH).astype(BF16)
    ymla = _rms(omla_ref[0].astype(F32), gmla_ref[...], 256).astype(BF16)
    mix = jnp.dot(yssd_ref[0], wout_ref[0:512, :], preferred_element_type=F32)
    mix = mix + jnp.dot(ysb, wout_ref[512:768, :], preferred_element_type=F32)
    mix = mix + jnp.dot(ymla, wout_ref[768:1024, :], preferred_element_type=F32)
    h1 = h_ref[0] + mix
    o_ref[0] = h1
    xn = _rms(h1, gffn_ref[...], D_MODEL).astype(BF16)

    w2 = 2 * FF_CHUNK
    for c in range(D_FF // FF_CHUNK):
        cols = slice(c * w2, (c + 1) * w2)
        ext_ref[0:8, :] = halo_ref[:, cols]
        ext_ref[8:8 + tm, :] = jnp.dot(xn, wup_ref[:, cols], preferred_element_type=F32)
        halo_ref[:, cols] = ext_ref[tm:tm + 8, :]
        cw = cw_ref[:, cols]
        conv = cb_ref[:, cols] + ext_ref[8:8 + tm, :] * cw[2:3]
        conv = conv + ext_ref[7:7 + tm, :] * cw[1:2]
        conv = conv + ext_ref[6:6 + tm, :] * cw[0:1]
        act = _silu(conv[:, :FF_CHUNK]) * conv[:, FF_CHUNK:]
        act_ref[:, c * FF_CHUNK:(c + 1) * FF_CHUNK] = act.astype(BF16)
    o_ref[0] = o_ref[0] + jnp.dot(act_ref[...], wdown_ref[...], preferred_element_type=F32)


def _out_ffn(h3, yssd, osb, omla, gsb, gmla, wout, gffn, wup, cw, cb, wdown):
    bsz, lp, _ = h3.shape
    tm = _pick(lp, (768, 256))
    const = lambda a: pl.BlockSpec(a.shape, lambda b, t: (0, 0), pipeline_mode=pl.Buffered(1))
    rowblk = lambda n: pl.BlockSpec((1, tm, n), lambda b, t: (b, t, 0))
    return pl.pallas_call(
        functools.partial(_out_ffn_kernel, tm=tm),
        grid=(bsz, lp // tm),
        in_specs=[rowblk(D_MODEL), rowblk(SSD_WIDTH), rowblk(256), rowblk(256),
                  const(gsb), const(gmla), const(wout), const(gffn),
                  const(wup), const(cw), const(cb), const(wdown)],
        out_specs=rowblk(D_MODEL),
        out_shape=jax.ShapeDtypeStruct(h3.shape, F32),
        scratch_shapes=[pltpu.VMEM((8, 2 * D_FF), F32),
                        pltpu.VMEM((tm + 8, 2 * FF_CHUNK), F32),
                        pltpu.VMEM((tm, D_FF), BF16)],
        compiler_params=_params("parallel", "arbitrary"),
        name="out_ffn",
    )(h3, yssd, osb, omla, gsb, gmla, wout, gffn, wup, cw, cb, wdown)


def _final_kernel(a_ref, t_ref, g_ref, o_ref, *, tm):
    g = g_ref[...]
    o_ref[0, 0:tm - N_META, :] = _rms(a_ref[0, N_META:tm, :], g, D_MODEL)
    o_ref[0, tm - N_META:tm, :] = _rms(t_ref[0], g, D_MODEL)


def _final_norm(h3, g, seq):
    bsz = h3.shape[0]
    tm = _pick(seq, (512, 128))
    return pl.pallas_call(
        functools.partial(_final_kernel, tm=tm),
        grid=(bsz, seq // tm),
        in_specs=[
            pl.BlockSpec((1, tm, D_MODEL), lambda b, i: (b, i, 0)),
            pl.BlockSpec((1, N_META, D_MODEL), lambda b, i: (b, (i + 1) * (tm // N_META), 0)),
            pl.BlockSpec((1, D_MODEL), lambda b, i: (0, 0)),
        ],
        out_specs=pl.BlockSpec((1, tm, D_MODEL), lambda b, i: (b, i, 0)),
        out_shape=jax.ShapeDtypeStruct((bsz, seq, D_MODEL), F32),
        compiler_params=_params("parallel", "parallel"),
        name="final_norm",
    )(h3, h3, g)


def _rot_cols(w):
    half = w.shape[1] // 2
    return jnp.concatenate([-w[:, half:], w[:, :half]], axis=1)


def _prep_in_proj(w_in):
    zeros = lambda n: jnp.zeros((D_MODEL, n), F32)
    cuts = (512, 1536, 1544, 1800, 2056, 2312, 2504, 2632)
    z, xbc, dt, q_sb, k_sb, v_sb, q_a, c_kv, k_r = jnp.split(w_in, cuts, axis=1)
    kr1 = jnp.concatenate([zeros(MLA_NOPE), k_r, zeros(LANES - MLA_NOPE - MLA_ROPE)], axis=1)
    kr2 = jnp.concatenate([zeros(MLA_NOPE), _rot_cols(k_r), zeros(LANES - MLA_NOPE - MLA_ROPE)], axis=1)
    w = jnp.concatenate([xbc, z, q_sb * (64 ** -0.5), k_sb, v_sb, q_a, zeros(256 - MLA_Q_RANK), c_kv,
                         kr1, kr2, jnp.repeat(dt, SSD_HEAD_DIM, axis=1)], axis=1)
    return w.astype(BF16)


def _prep_mla(w_uq, w_ukv):
    qh = w_uq.reshape(MLA_Q_RANK, MLA_HEADS, MLA_NOPE + MLA_ROPE)
    pad = jnp.zeros((MLA_Q_RANK, MLA_HEADS, LANES - MLA_NOPE - MLA_ROPE), F32)
    nope0 = jnp.zeros((MLA_Q_RANK, MLA_HEADS, MLA_NOPE), F32)
    rope = qh[..., MLA_NOPE:]
    rope_rot = jnp.concatenate([-rope[..., MLA_ROPE // 2:], rope[..., :MLA_ROPE // 2]], axis=-1)
    wq = jnp.concatenate([qh, pad], axis=-1).reshape(MLA_Q_RANK, MLA_HEADS * LANES)
    wqr = jnp.concatenate([nope0, rope_rot, pad], axis=-1).reshape(MLA_Q_RANK, MLA_HEADS * LANES)
    rowpad = jnp.zeros((256 - MLA_Q_RANK, MLA_HEADS * LANES), F32)
    wq = jnp.concatenate([wq, rowpad], axis=0).astype(BF16)
    wqr = jnp.concatenate([wqr, rowpad], axis=0).astype(BF16)
    kvh = w_ukv.reshape(MLA_KV_RANK, MLA_HEADS, MLA_NOPE + 64)
    kpad = jnp.zeros((MLA_KV_RANK, MLA_HEADS, LANES - MLA_NOPE), F32)
    wk = jnp.concatenate([kvh[..., :MLA_NOPE], kpad], axis=-1).reshape(MLA_KV_RANK, MLA_HEADS * LANES)
    wvt = kvh[..., MLA_NOPE:].reshape(MLA_KV_RANK, MLA_HEADS * 64).T
    return wq, wqr, wk.astype(BF16), wvt.astype(BF16)


def _interleave_ff(a):
    lead = a.shape[:-1]
    g = a[..., :D_FF].reshape(lead + (D_FF // FF_CHUNK, FF_CHUNK))
    v = a[..., D_FF:].reshape(lead + (D_FF // FF_CHUNK, FF_CHUNK))
    return jnp.concatenate([g, v], axis=-1).reshape(lead + (2 * D_FF,))


def _rope_tables(lp):
    pos = jnp.arange(lp, dtype=F32)
    inv = 1.0 / (ROPE_BASE ** (jnp.arange(0, MLA_ROPE, 2, dtype=F32) / MLA_ROPE))
    ang = pos[:, None] * inv[None, :]
    ang = jnp.concatenate([ang, ang], axis=-1)
    ones = jnp.ones((lp, MLA_NOPE), F32)
    tail = LANES - MLA_NOPE - MLA_ROPE
    cosx = jnp.concatenate([ones, jnp.cos(ang), jnp.ones((lp, tail), F32)], axis=1)
    sinx = jnp.concatenate([0 * ones, jnp.sin(ang), jnp.zeros((lp, tail), F32)], axis=1)
    return cosx, sinx


def kernel(x, meta_tokens, norm_mix_g, w_in, ssd_conv_w, ssd_conv_b, ssd_dt_bias, ssd_a_log, ssd_d, ssd_norm_g, sb_norm_g, mla_q_norm_g, mla_kv_norm_g, mla_w_uq, mla_w_ukv, mla_norm_g, w_out, norm_ffn_g, ffn_w_up, ffn_conv_w, ffn_conv_b, ffn_w_down, final_norm_g):
    bsz, seq, _ = x.shape
    length = N_META + seq
    lp = -(-length // SEQ_ALIGN) * SEQ_ALIGN
    depth = w_in.shape[0]

    meta = jnp.broadcast_to(meta_tokens[None].astype(x.dtype), (bsz, N_META, D_MODEL))
    h = jnp.concatenate([meta, x, jnp.zeros((bsz, lp - length, D_MODEL), x.dtype)], axis=1)

    cosx, sinx = _rope_tables(lp)
    later_t = (lax.broadcasted_iota(jnp.int32, (ATT_SUB, ATT_SUB), 1) >
               lax.broadcasted_iota(jnp.int32, (ATT_SUB, ATT_SUB), 0)).astype(BF16)
    row = lambda v: v.reshape(1, -1)
    rep = lambda v: jnp.repeat(v, SSD_HEAD_DIM).reshape(1, -1)

    for l in range(depth):
        u, dtx = _in_proj(h.reshape(bsz * lp, D_MODEL), row(norm_mix_g[l]), _prep_in_proj(w_in[l]))
        u3 = u.reshape(bsz, lp, U_END)
        y_ssd = _ssd(u3, dtx.reshape(bsz, lp, SSD_WIDTH), ssd_conv_w[l], row(ssd_conv_b[l]),
                     rep(ssd_dt_bias[l]), rep(ssd_a_log[l]), rep(ssd_d[l]), row(ssd_norm_g[l]))
        vt_sb = jnp.swapaxes(u3[:, :, U_VSB:U_VSB + SB_WIDTH], 1, 2)
        o_sb = _sb_attention(u3, vt_sb, later_t)
        wq, wqr, wk, wvt = _prep_mla(mla_w_uq[l], mla_w_ukv[l])
        gq = jnp.concatenate([mla_q_norm_g[l], jnp.zeros((256 - MLA_Q_RANK,), F32)]).reshape(1, -1)
        qc, kc, vt_mla = _mla_proj(u, cosx, sinx, gq, row(mla_kv_norm_g[l]), wq, wqr, wk, wvt, bsz, lp)
        o_mla = _mla_attention(qc.reshape(bsz, lp, -1), kc.reshape(bsz, lp, -1), vt_mla)
        h = _out_ffn(h, y_ssd, o_sb, o_mla, row(sb_norm_g[l]), row(mla_norm_g[l]),
                     w_out[l].astype(BF16), row(norm_ffn_g[l]),
                     _interleave_ff(ffn_w_up[l]).astype(BF16), _interleave_ff(ffn_conv_w[l]),
                     row(_interleave_ff(ffn_conv_b[l])), ffn_w_down[l].astype(BF16))

    return _final_norm(h, row(final_norm_g), seq)
```

```python
import functools
import math

import jax
import jax.numpy as jnp
from jax import lax
from jax.experimental import pallas as pl
from jax.experimental.pallas import tpu as pltpu

F32 = jnp.float32
BF16 = jnp.bfloat16

D_MODEL = 1024
N_META = 16
EPS = 1e-6
SSD_HEADS = 8
SSD_HEAD_DIM = 64
SSD_WIDTH = 512
SSD_STATE = 128
SSD_XBC = 1024
SSD_CHUNK = 128
SB_HEADS = 4
SB_HEAD_DIM = 64
SB_WIDTH = 256
MLA_HEADS = 4
MLA_V = 64
MLA_NOPE = 64
MLA_ROPE = 32
MLA_Q_RANK = 192
MLA_KV_RANK = 128
ROPE_BASE = 10000.0
D_FF = 2816
FF_CHUNK = 256

LANES = 128
ATT_SUB = 256
SEQ_ALIGN = ATT_SUB
VMEM_LIMIT = 56 * 1024 * 1024

U_XBC, U_Z, U_QSB, U_KSB, U_VSB, U_QA, U_CKV, U_KR1, U_KR2, U_END = (
    0, 1024, 1536, 1792, 2048, 2304, 2560, 2688, 2816, 2944)

NT_DIMS = (((1,), (1,)), ((), ()))
LOG2E = 1.4426950408889634


def _pick(n, candidates):
    for c in candidates:
        if n % c == 0:
            return c
    raise ValueError(f"no tile in {candidates} divides {n}")


def _softplus(x):
    return jnp.maximum(x, 0.0) + jnp.log(1.0 + jnp.exp2(jnp.abs(x) * (-LOG2E)))


def _silu(x):
    return x * jax.nn.sigmoid(x)


def _rms(x, g, n):
    ms = jnp.sum(x * x, axis=-1, keepdims=True) * (1.0 / n)
    return (x * lax.rsqrt(ms + EPS)) * g


def _params(*sem):
    return pltpu.CompilerParams(dimension_semantics=sem, vmem_limit_bytes=VMEM_LIMIT)


def _in_proj_kernel(x_ref, g_ref, w_ref, u_ref, dt_ref):
    xn = _rms(x_ref[...], g_ref[...], D_MODEL).astype(BF16)
    for n0 in range(0, U_END, 512):
        n1 = min(n0 + 512, U_END)
        u_ref[:, n0:n1] = jnp.dot(xn, w_ref[:, n0:n1], preferred_element_type=F32).astype(BF16)
    dt_ref[...] = jnp.dot(xn, w_ref[:, U_END:U_END + SSD_WIDTH], preferred_element_type=F32)


def _in_proj(h2d, g, w):
    rows = h2d.shape[0]
    tm = _pick(rows, (512, 256))
    return pl.pallas_call(
        _in_proj_kernel,
        grid=(rows // tm,),
        in_specs=[
            pl.BlockSpec((tm, D_MODEL), lambda i: (i, 0)),
            pl.BlockSpec((1, D_MODEL), lambda i: (0, 0)),
            pl.BlockSpec(w.shape, lambda i: (0, 0), pipeline_mode=pl.Buffered(1)),
        ],
        out_specs=[
            pl.BlockSpec((tm, U_END), lambda i: (i, 0)),
            pl.BlockSpec((tm, SSD_WIDTH), lambda i: (i, 0)),
        ],
        out_shape=[
            jax.ShapeDtypeStruct((rows, U_END), BF16),
            jax.ShapeDtypeStruct((rows, SSD_WIDTH), F32),
        ],
        compiler_params=_params("parallel"),
        name="in_proj",
    )(h2d, g, w)


def _cumsum_rows(x):
    n = x.shape[0]
    row = lax.broadcasted_iota(jnp.int32, x.shape, 0)
    s = 1
    while s < n:
        x = x + jnp.where(row >= s, pltpu.roll(x, s, 0), 0.0)
        s *= 2
    return x


def _ssd_kernel(z_ref, xbc_ref, dt_ref, cw_ref, cb_ref, dtb_ref, alog_ref, d_ref, g_ref,
                y_ref, ext_ref, st_ref):
    T = SSD_CHUNK

    @pl.when(pl.program_id(1) == 0)
    def _():
        ext_ref[0:8, :] = jnp.zeros((8, SSD_XBC), F32)
        st_ref[...] = jnp.zeros(st_ref.shape, F32)

    ext_ref[8:8 + T, :] = xbc_ref[0].astype(F32)
    cw = cw_ref[...]
    conv = cb_ref[...] + ext_ref[8:8 + T, :] * cw[3:4]
    for k in range(3):
        conv = conv + ext_ref[5 + k:5 + k + T, :] * cw[k:k + 1]
    ext_ref[0:8, :] = ext_ref[T:T + 8, :]
    xc = _silu(conv)
    xs = xc[:, :SSD_WIDTH]
    bm = xc[:, SSD_WIDTH:SSD_WIDTH + 2 * SSD_STATE]
    cm = xc[:, SSD_WIDTH + 2 * SSD_STATE:]

    dt = _softplus(dt_ref[0] + dtb_ref[...])
    acs = _cumsum_rows(dt * (-jnp.exp(alog_ref[...])))
    acs_last = acs[T - 1:T, :]
    x_dt = xs * dt
    xb = x_dt.astype(BF16)
    xd = (x_dt * jnp.exp(acs_last - acs)).astype(BF16)
    eacs = jnp.exp(acs)
    acs_t = acs.T

    tril = (lax.broadcasted_iota(jnp.int32, (T, T), 0) >= lax.broadcasted_iota(jnp.int32, (T, T), 1))
    low_half = lax.broadcasted_iota(jnp.int32, (T, LANES), 1) < SSD_HEAD_DIM
    y_groups = []
    for g in range(2):
        bg = bm[:, g * SSD_STATE:(g + 1) * SSD_STATE]
        cg = cm[:, g * SSD_STATE:(g + 1) * SSD_STATE].astype(BF16)
        cb = lax.dot_general(cg, bg.astype(BF16), NT_DIMS, preferred_element_type=F32)
        pair_out = []
        for pr in range(2):
            lane0 = g * 256 + pr * LANES
            xp = xb[:, lane0:lane0 + LANES]
            res = []
            for hh in range(2):
                hl = lane0 + hh * SSD_HEAD_DIM
                seg = acs[:, hl:hl + 1] - acs_t[hl:hl + 1, :]
                decay = jnp.where(tril, jnp.exp(seg), 0.0)
                res.append(jnp.dot((cb * decay).astype(BF16), xp, preferred_element_type=F32))
            pair_out.append(jnp.where(low_half, res[0], res[1]))
        y_diag = jnp.concatenate(pair_out, axis=1)
        hg = st_ref[g]
        y_off = jnp.dot(cg, hg.astype(BF16), preferred_element_type=F32) * eacs[:, g * 256:(g + 1) * 256]
        st_ref[g] = hg * jnp.exp(acs_last[:, g * 256:(g + 1) * 256]) + jnp.dot(
            bg.T.astype(BF16), xd[:, g * 256:(g + 1) * 256], preferred_element_type=F32)
        y_groups.append(y_diag + y_off)
    y = jnp.concatenate(y_groups, axis=1) + xs * d_ref[...]
    y = y * _silu(z_ref[0].astype(F32))
    y_ref[0] = _rms(y, g_ref[...], SSD_WIDTH).astype(BF16)


def _ssd(u3, dtx3, cw, cb, dtb, alog, dskip, g):
    bsz, lp, _ = u3.shape
    T = SSD_CHUNK
    vec = lambda n: pl.BlockSpec((1, n), lambda b, c: (0, 0))
    return pl.pallas_call(
        _ssd_kernel,
        grid=(bsz, lp // T),
        in_specs=[
            pl.BlockSpec((1, T, SSD_WIDTH), lambda b, c: (b, c, U_Z // SSD_WIDTH)),
            pl.BlockSpec((1, T, SSD_XBC), lambda b, c: (b, c, U_XBC // SSD_XBC)),
            pl.BlockSpec((1, T, SSD_WIDTH), lambda b, c: (b, c, 0)),
            pl.BlockSpec((4, SSD_XBC), lambda b, c: (0, 0)),
            vec(SSD_XBC), vec(SSD_WIDTH), vec(SSD_WIDTH), vec(SSD_WIDTH), vec(SSD_WIDTH),
        ],
        out_specs=pl.BlockSpec((1, T, SSD_WIDTH), lambda b, c: (b, c, 0)),
        out_shape=jax.ShapeDtypeStruct((bsz, lp, SSD_WIDTH), BF16),
        scratch_shapes=[pltpu.VMEM((T + 8, SSD_XBC), F32), pltpu.VMEM((2, SSD_STATE, 256), F32)],
        compiler_params=_params("parallel", "arbitrary"),
        name="ssd_scan",
    )(u3, u3, dtx3, cw, cb, dtb, alog, dskip, g)


def _key_start(qi, n):
    return pl.multiple_of(jnp.maximum(qi - n, 0) * ATT_SUB, ATT_SUB)


def _head_queries(q, heads):
    lane = lax.broadcasted_iota(jnp.int32, (q.shape[0], LANES), 1)
    out = []
    for h in range(heads):
        pair = q[:, (h // 2) * LANES:(h // 2 + 1) * LANES].astype(F32)
        lo = 64 * (h % 2)
        out.append(jnp.where((lane >= lo) & (lane < lo + 64), pair, 0.0).astype(BF16))
    return out


def _pipeline3(nblk, stage1, stage2, stage3):
    stage1(0, 0)
    stage2(0, True)
    stage1(1, 1)
    stage3(0, 0, True)
    stage2(1, False, nblk >= 2)
    stage1(2, 0)

    def step(n, par):
        stage1(n + 2, par)
        stage3(n, par, False)
        stage2(1 - par, False)

    steady = jnp.maximum(nblk - 3, 0)

    def body(i, c):
        step(1 + 2 * i, 1)
        step(2 + 2 * i, 0)
        return c

    lax.fori_loop(0, steady // 2, body, 0)

    @pl.when(steady % 2 == 1)
    def _():
        step(nblk - 3, 1)

    @pl.when(nblk >= 3)
    def _():
        par = (nblk - 2) % 2
        stage3(nblk - 2, par, False)
        stage2(1 - par, False)
        stage3(nblk - 1, 1 - par, False)

    @pl.when(nblk == 2)
    def _():
        stage3(1, 1, False)


def _sb_kernel(q_ref, k_ref, vt_ref, lt_ref, o_ref, s_scr, t_scr, tot_scr, acc_scr, r_scr):
    qi = pl.program_id(1)
    nblk = qi + 1
    qms = _head_queries(q_ref[0], SB_HEADS)
    later_t = lt_ref[...]
    key_i = lax.broadcasted_iota(jnp.int32, (ATT_SUB, ATT_SUB), 0)
    qry_i = lax.broadcasted_iota(jnp.int32, (ATT_SUB, ATT_SUB), 1)
    valid = key_i < qry_i
    pair = lambda h: slice((h // 2) * LANES, (h // 2 + 1) * LANES)
    rows = lambda h: slice(h * SB_HEAD_DIM, (h + 1) * SB_HEAD_DIM)

    def stage1(n, par):
        ks = _key_start(qi, n)
        for h in range(SB_HEADS):
            s_scr[par, h] = lax.dot_general(k_ref[0, pl.ds(ks, ATT_SUB), pair(h)], qms[h], NT_DIMS,
                                            preferred_element_type=F32)

    def stage2(par, diagonal, exists=None):
        for h in range(SB_HEADS):
            s = s_scr[par, h]
            sp = _softplus(s)
            if diagonal:
                sp = jnp.where(valid, sp, 0.0)
            loc = jnp.dot(later_t, sp.astype(BF16), preferred_element_type=F32)
            t = s - loc
            t_scr[par, h] = jnp.where(valid, t, -1e30) if diagonal else t
            tot_scr[par, h] = loc[0:1, :]

    def stage3(n, par, first):
        ks = _key_start(qi, n)
        for h in range(SB_HEADS):
            w = jnp.exp(t_scr[par, h]).astype(BF16)
            pv = jnp.dot(vt_ref[0, rows(h), pl.ds(ks, ATT_SUB)], w, preferred_element_type=F32)
            if first:
                acc_scr[rows(h), :] = pv
                r_scr[h] = -tot_scr[par, h]
            else:
                r = r_scr[h]
                acc_scr[rows(h), :] = acc_scr[rows(h), :] + jnp.exp(r) * pv
                r_scr[h] = r - tot_scr[par, h]

    _pipeline3(nblk, stage1, stage2, stage3)
    o_ref[0] = acc_scr[...].T.astype(BF16)


def _sb_attention(u3, vt, later_t):
    bsz, lp, _ = u3.shape
    tq = ATT_SUB
    return pl.pallas_call(
        _sb_kernel,
        grid=(bsz, lp // tq),
        in_specs=[
            pl.BlockSpec((1, tq, SB_WIDTH), lambda b, i: (b, i, U_QSB // SB_WIDTH)),
            pl.BlockSpec((1, lp, SB_WIDTH), lambda b, i: (b, 0, U_KSB // SB_WIDTH)),
            pl.BlockSpec((1, SB_WIDTH, lp), lambda b, i: (b, 0, 0)),
            pl.BlockSpec((ATT_SUB, ATT_SUB), lambda b, i: (0, 0)),
        ],
        out_specs=pl.BlockSpec((1, tq, SB_WIDTH), lambda b, i: (b, i, 0)),
        out_shape=jax.ShapeDtypeStruct((bsz, lp, SB_WIDTH), BF16),
        scratch_shapes=[pltpu.VMEM((2, SB_HEADS, ATT_SUB, tq), F32), pltpu.VMEM((2, SB_HEADS, ATT_SUB, tq), F32),
                        pltpu.VMEM((2, SB_HEADS, 1, tq), F32), pltpu.VMEM((SB_WIDTH, tq), F32),
                        pltpu.VMEM((SB_HEADS, 1, tq), F32)],
        compiler_params=_params("parallel", "arbitrary"),
        name="sb_attention",
    )(u3, u3, vt, later_t)


def _mla_proj_kernel(qa_ref, ckv_ref, kr1_ref, kr2_ref, cos_ref, sin_ref, gq_ref, gkv_ref,
                     wq_ref, wqr_ref, wk_ref, wvt_ref, qc_ref, kc_ref, vt_ref):
    cos = cos_ref[...]
    sin = sin_ref[...]
    cos4 = jnp.concatenate([cos] * MLA_HEADS, axis=1)
    sin4 = jnp.concatenate([sin] * MLA_HEADS, axis=1)
    qn = _rms(qa_ref[...].astype(F32), gq_ref[...], MLA_Q_RANK).astype(BF16)
    q1 = jnp.dot(qn, wq_ref[...], preferred_element_type=F32)
    q2 = jnp.dot(qn, wqr_ref[...], preferred_element_type=F32)
    scale = (MLA_NOPE + MLA_ROPE) ** -0.5
    qc_ref[...] = ((q1 * cos4 + q2 * sin4) * scale).astype(BF16)
    cn = _rms(ckv_ref[...].astype(F32), gkv_ref[...], MLA_KV_RANK).astype(BF16)
    kn = jnp.dot(cn, wk_ref[...], preferred_element_type=F32)
    krope = kr1_ref[...].astype(F32) * cos + kr2_ref[...].astype(F32) * sin
    kc_ref[...] = (kn + jnp.concatenate([krope] * MLA_HEADS, axis=1)).astype(BF16)
    vt_ref[0] = lax.dot_general(wvt_ref[...], cn, NT_DIMS, preferred_element_type=F32).astype(BF16)


def _mla_proj(u2d, cosx, sinx, gq, gkv, wq, wqr, wk, wvt, bsz, lp):
    rows = u2d.shape[0]
    tm = _pick(lp, (768, 512, 256))
    nper = lp // tm
    const = lambda a: pl.BlockSpec(a.shape, lambda i: (0, 0))
    return pl.pallas_call(
        _mla_proj_kernel,
        grid=(rows // tm,),
        in_specs=[
            pl.BlockSpec((tm, 256), lambda i: (i, U_QA // 256)),
            pl.BlockSpec((tm, LANES), lambda i: (i, U_CKV // LANES)),
            pl.BlockSpec((tm, LANES), lambda i: (i, U_KR1 // LANES)),
            pl.BlockSpec((tm, LANES), lambda i: (i, U_KR2 // LANES)),
            pl.BlockSpec((tm, LANES), lambda i: (i % nper, 0)),
            pl.BlockSpec((tm, LANES), lambda i: (i % nper, 0)),
            const(gq), const(gkv), const(wq), const(wqr), const(wk), const(wvt),
        ],
        out_specs=[
            pl.BlockSpec((tm, MLA_HEADS * LANES), lambda i: (i, 0)),
            pl.BlockSpec((tm, MLA_HEADS * LANES), lambda i: (i, 0)),
            pl.BlockSpec((1, 256, tm), lambda i: (i // nper, 0, i % nper)),
        ],
        out_shape=[
            jax.ShapeDtypeStruct((rows, MLA_HEADS * LANES), BF16),
            jax.ShapeDtypeStruct((rows, MLA_HEADS * LANES), BF16),
            jax.ShapeDtypeStruct((bsz, 256, lp), BF16),
        ],
        compiler_params=_params("parallel"),
        name="mla_proj",
    )(u2d, u2d, u2d, u2d, cosx, sinx, gq, gkv, wq, wqr, wk, wvt)


def _mla_kernel(q_ref, k_ref, vt_ref, o_ref, s_scr, p_scr, alpha_scr, acc_scr, m_scr, l_scr):
    qi = pl.program_id(1)
    nblk = qi + 1
    qs = [q_ref[0, :, h * LANES:(h + 1) * LANES] for h in range(MLA_HEADS)]
    key_i = lax.broadcasted_iota(jnp.int32, (ATT_SUB, ATT_SUB), 0)
    qry_i = lax.broadcasted_iota(jnp.int32, (ATT_SUB, ATT_SUB), 1)
    valid = key_i <= qry_i
    rows = lambda h: slice(h * MLA_V, (h + 1) * MLA_V)

    def stage1(n, par):
        ks = _key_start(qi, n)
        for h in range(MLA_HEADS):
            s_scr[par, h] = lax.dot_general(k_ref[0, pl.ds(ks, ATT_SUB), h * LANES:(h + 1) * LANES], qs[h],
                                            NT_DIMS, preferred_element_type=F32)

    def stage2(par, diagonal, exists=None):
        for h in range(MLA_HEADS):
            s = s_scr[par, h]
            if diagonal:
                s = jnp.where(valid, s, -1e30)
                m_new = jnp.max(s, axis=0, keepdims=True)
            else:
                m_old = m_scr[h]
                m_new = jnp.maximum(m_old, jnp.max(s, axis=0, keepdims=True))
            p = jnp.exp(s - m_new)
            p_scr[par, h] = p.astype(BF16)
            l_new = jnp.sum(p, axis=0, keepdims=True)
            if not diagonal:
                alpha = jnp.exp(m_old - m_new)
                alpha_scr[par, h] = alpha
                l_new = alpha * l_scr[h] + l_new
                if exists is not None:
                    l_new = jnp.where(exists, l_new, l_scr[h])
                    m_new = jnp.where(exists, m_new, m_old)
            l_scr[h] = l_new
            m_scr[h] = m_new

    def stage3(n, par, first):
        ks = _key_start(qi, n)
        for h in range(MLA_HEADS):
            pv = jnp.dot(vt_ref[0, rows(h), pl.ds(ks, ATT_SUB)], p_scr[par, h], preferred_element_type=F32)
            if first:
                acc_scr[rows(h), :] = pv
            else:
                acc_scr[rows(h), :] = alpha_scr[par, h] * acc_scr[rows(h), :] + pv

    _pipeline3(nblk, stage1, stage2, stage3)
    for h in range(MLA_HEADS):
        acc_scr[rows(h), :] = acc_scr[rows(h), :] / l_scr[h]
    o_ref[0] = acc_scr[...].T.astype(BF16)


def _mla_attention(qc3, kc3, vt):
    bsz, lp, width = qc3.shape
    tq = ATT_SUB
    out_w = MLA_HEADS * MLA_V
    return pl.pallas_call(
        _mla_kernel,
        grid=(bsz, lp // tq),
        in_specs=[
            pl.BlockSpec((1, tq, width), lambda b, i: (b, i, 0)),
            pl.BlockSpec((1, lp, width), lambda b, i: (b, 0, 0)),
            pl.BlockSpec((1, out_w, lp), lambda b, i: (b, 0, 0)),
        ],
        out_specs=pl.BlockSpec((1, tq, out_w), lambda b, i: (b, i, 0)),
        out_shape=jax.ShapeDtypeStruct((bsz, lp, out_w), BF16),
        scratch_shapes=[pltpu.VMEM((2, MLA_HEADS, ATT_SUB, tq), F32), pltpu.VMEM((2, MLA_HEADS, ATT_SUB, tq), BF16),
                        pltpu.VMEM((2, MLA_HEADS, 1, tq), F32), pltpu.VMEM((out_w, tq), F32),
                        pltpu.VMEM((MLA_HEADS, 1, tq), F32), pltpu.VMEM((MLA_HEADS, 1, tq), F32)],
        compiler_params=_params("parallel", "arbitrary"),
        name="mla_attention",
    )(qc3, kc3, vt)


def _out_ffn_kernel(h_ref, yssd_ref, osb_ref, omla_ref, gsb_ref, gmla_ref, wout_ref, gffn_ref,
                    wup_ref, cw_ref, cb_ref, wdown_ref, o_ref, halo_ref, ext_ref, act_ref, *, tm):
    @pl.when(pl.program_id(1) == 0)
    def _():
        halo_ref[...] = jnp.zeros(halo_ref.shape, F32)

    ysb = _rms(osb_ref[0].astype(F32), gsb_ref[...], SB_WIDTH).astype(BF16)
    ymla = _rms(omla_ref[0].astype(F32), gmla_ref[...], 256).astype(BF16)
    mix = jnp.dot(yssd_ref[0], wout_ref[0:512, :], preferred_element_type=F32)
    mix = mix + jnp.dot(ysb, wout_ref[512:768, :], preferred_element_type=F32)
    mix = mix + jnp.dot(ymla, wout_ref[768:1024, :], preferred_element_type=F32)
    h1 = h_ref[0] + mix
    o_ref[0] = h1
    xn = _rms(h1, gffn_ref[...], D_MODEL).astype(BF16)

    w2 = 2 * FF_CHUNK
    for c in range(D_FF // FF_CHUNK):
        cols = slice(c * w2, (c + 1) * w2)
        ext_ref[0:8, :] = halo_ref[:, cols]
        ext_ref[8:8 + tm, :] = jnp.dot(xn, wup_ref[:, cols], preferred_element_type=F32)
        halo_ref[:, cols] = ext_ref[tm:tm + 8, :]
        cw = cw_ref[:, cols]
        conv = cb_ref[:, cols] + ext_ref[8:8 + tm, :] * cw[2:3]
        conv = conv + ext_ref[7:7 + tm, :] * cw[1:2]
        conv = conv + ext_ref[6:6 + tm, :] * cw[0:1]
        act = _silu(conv[:, :FF_CHUNK]) * conv[:, FF_CHUNK:]
        act_ref[:, c * FF_CHUNK:(c + 1) * FF_CHUNK] = act.astype(BF16)
    o_ref[0] = o_ref[0] + jnp.dot(act_ref[...], wdown_ref[...], preferred_element_type=F32)


def _out_ffn(h3, yssd, osb, omla, gsb, gmla, wout, gffn, wup, cw, cb, wdown):
    bsz, lp, _ = h3.shape
    tm = _pick(lp, (768, 256))
    const = lambda a: pl.BlockSpec(a.shape, lambda b, t: (0, 0), pipeline_mode=pl.Buffered(1))
    rowblk = lambda n: pl.BlockSpec((1, tm, n), lambda b, t: (b, t, 0))
    return pl.pallas_call(
        functools.partial(_out_ffn_kernel, tm=tm),
        grid=(bsz, lp // tm),
        in_specs=[rowblk(D_MODEL), rowblk(SSD_WIDTH), rowblk(256), rowblk(256),
                  const(gsb), const(gmla), const(wout), const(gffn),
                  const(wup), const(cw), const(cb), const(wdown)],
        out_specs=rowblk(D_MODEL),
        out_shape=jax.ShapeDtypeStruct(h3.shape, F32),
        scratch_shapes=[pltpu.VMEM((8, 2 * D_FF), F32),
                        pltpu.VMEM((tm + 8, 2 * FF_CHUNK), F32),
                        pltpu.VMEM((tm, D_FF), BF16)],
        compiler_params=_params("parallel", "arbitrary"),
        name="out_ffn",
    )(h3, yssd, osb, omla, gsb, gmla, wout, gffn, wup, cw, cb, wdown)


def _final_kernel(a_ref, t_ref, g_ref, o_ref, *, tm):
    g = g_ref[...]
    o_ref[0, 0:tm - N_META, :] = _rms(a_ref[0, N_META:tm, :], g, D_MODEL)
    o_ref[0, tm - N_META:tm, :] = _rms(t_ref[0], g, D_MODEL)


def _final_norm(h3, g, seq):
    bsz = h3.shape[0]
    tm = _pick(seq, (512, 128))
    return pl.pallas_call(
        functools.partial(_final_kernel, tm=tm),
        grid=(bsz, seq // tm),
        in_specs=[
            pl.BlockSpec((1, tm, D_MODEL), lambda b, i: (b, i, 0)),
            pl.BlockSpec((1, N_META, D_MODEL), lambda b, i: (b, (i + 1) * (tm // N_META), 0)),
            pl.BlockSpec((1, D_MODEL), lambda b, i: (0, 0)),
        ],
        out_specs=pl.BlockSpec((1, tm, D_MODEL), lambda b, i: (b, i, 0)),
        out_shape=jax.ShapeDtypeStruct((bsz, seq, D_MODEL), F32),
        compiler_params=_params("parallel", "parallel"),
        name="final_norm",
    )(h3, h3, g)


def _rot_cols(w):
    half = w.shape[1] // 2
    return jnp.concatenate([-w[:, half:], w[:, :half]], axis=1)


def _prep_in_proj(w_in):
    zeros = lambda n: jnp.zeros((D_MODEL, n), F32)
    cuts = (512, 1536, 1544, 1800, 2056, 2312, 2504, 2632)
    z, xbc, dt, q_sb, k_sb, v_sb, q_a, c_kv, k_r = jnp.split(w_in, cuts, axis=1)
    kr1 = jnp.concatenate([zeros(MLA_NOPE), k_r, zeros(LANES - MLA_NOPE - MLA_ROPE)], axis=1)
    kr2 = jnp.concatenate([zeros(MLA_NOPE), _rot_cols(k_r), zeros(LANES - MLA_NOPE - MLA_ROPE)], axis=1)
    w = jnp.concatenate([xbc, z, q_sb * (64 ** -0.5), k_sb, v_sb, q_a, zeros(256 - MLA_Q_RANK), c_kv,
                         kr1, kr2, jnp.repeat(dt, SSD_HEAD_DIM, axis=1)], axis=1)
    return w.astype(BF16)


def _prep_mla(w_uq, w_ukv):
    qh = w_uq.reshape(MLA_Q_RANK, MLA_HEADS, MLA_NOPE + MLA_ROPE)
    pad = jnp.zeros((MLA_Q_RANK, MLA_HEADS, LANES - MLA_NOPE - MLA_ROPE), F32)
    nope0 = jnp.zeros((MLA_Q_RANK, MLA_HEADS, MLA_NOPE), F32)
    rope = qh[..., MLA_NOPE:]
    rope_rot = jnp.concatenate([-rope[..., MLA_ROPE // 2:], rope[..., :MLA_ROPE // 2]], axis=-1)
    wq = jnp.concatenate([qh, pad], axis=-1).reshape(MLA_Q_RANK, MLA_HEADS * LANES)
    wqr = jnp.concatenate([nope0, rope_rot, pad], axis=-1).reshape(MLA_Q_RANK, MLA_HEADS * LANES)
    rowpad = jnp.zeros((256 - MLA_Q_RANK, MLA_HEADS * LANES), F32)
    wq = jnp.concatenate([wq, rowpad], axis=0).astype(BF16)
    wqr = jnp.concatenate([wqr, rowpad], axis=0).astype(BF16)
    kvh = w_ukv.reshape(MLA_KV_RANK, MLA_HEADS, MLA_NOPE + 64)
    kpad = jnp.zeros((MLA_KV_RANK, MLA_HEADS, LANES - MLA_NOPE), F32)
    wk = jnp.concatenate([kvh[..., :MLA_NOPE], kpad], axis=-1).reshape(MLA_KV_RANK, MLA_HEADS * LANES)
    wvt = kvh[..., MLA_NOPE:].reshape(MLA_KV_RANK, MLA_HEADS * 64).T
    return wq, wqr, wk.astype(BF16), wvt.astype(BF16)


def _interleave_ff(a):
    lead = a.shape[:-1]
    g = a[..., :D_FF].reshape(lead + (D_FF // FF_CHUNK, FF_CHUNK))
    v = a[..., D_FF:].reshape(lead + (D_FF // FF_CHUNK, FF_CHUNK))
    return jnp.concatenate([g, v], axis=-1).reshape(lead + (2 * D_FF,))


def _rope_tables(lp):
    pos = jnp.arange(lp, dtype=F32)
    inv = 1.0 / (ROPE_BASE ** (jnp.arange(0, MLA_ROPE, 2, dtype=F32) / MLA_ROPE))
    ang = pos[:, None] * inv[None, :]
    ang = jnp.concatenate([ang, ang], axis=-1)
    ones = jnp.ones((lp, MLA_NOPE), F32)
    tail = LANES - MLA_NOPE - MLA_ROPE
    cosx = jnp.concatenate([ones, jnp.cos(ang), jnp.ones((lp, tail), F32)], axis=1)
    sinx = jnp.concatenate([0 * ones, jnp.sin(ang), jnp.zeros((lp, tail), F32)], axis=1)
    return cosx, sinx


def kernel(x, meta_tokens, norm_mix_g, w_in, ssd_conv_w, ssd_conv_b, ssd_dt_bias, ssd_a_log, ssd_d, ssd_norm_g, sb_norm_g, mla_q_norm_g, mla_kv_norm_g, mla_w_uq, mla_w_ukv, mla_norm_g, w_out, norm_ffn_g, ffn_w_up, ffn_conv_w, ffn_conv_b, ffn_w_down, final_norm_g):
    bsz, seq, _ = x.shape
    length = N_META + seq
    lp = -(-length // SEQ_ALIGN) * SEQ_ALIGN
    depth = w_in.shape[0]

    meta = jnp.broadcast_to(meta_tokens[None].astype(x.dtype), (bsz, N_META, D_MODEL))
    h = jnp.concatenate([meta, x, jnp.zeros((bsz, lp - length, D_MODEL), x.dtype)], axis=1)

    cosx, sinx = _rope_tables(lp)
    later_t = (lax.broadcasted_iota(jnp.int32, (ATT_SUB, ATT_SUB), 1) >=
               lax.broadcasted_iota(jnp.int32, (ATT_SUB, ATT_SUB), 0)).astype(BF16)
    row = lambda v: v.reshape(1, -1)
    rep = lambda v: jnp.repeat(v, SSD_HEAD_DIM).reshape(1, -1)

    for l in range(depth):
        u, dtx = _in_proj(h.reshape(bsz * lp, D_MODEL), row(norm_mix_g[l]), _prep_in_proj(w_in[l]))
        u3 = u.reshape(bsz, lp, U_END)
        y_ssd = _ssd(u3, dtx.reshape(bsz, lp, SSD_WIDTH), ssd_conv_w[l], row(ssd_conv_b[l]),
                     rep(ssd_dt_bias[l]), rep(ssd_a_log[l]), rep(ssd_d[l]), row(ssd_norm_g[l]))
        vt_sb = jnp.swapaxes(u3[:, :, U_VSB:U_VSB + SB_WIDTH], 1, 2)
        o_sb = _sb_attention(u3, vt_sb, later_t)
        wq, wqr, wk, wvt = _prep_mla(mla_w_uq[l], mla_w_ukv[l])
        gq = jnp.concatenate([mla_q_norm_g[l], jnp.zeros((256 - MLA_Q_RANK,), F32)]).reshape(1, -1)
        qc, kc, vt_mla = _mla_proj(u, cosx, sinx, gq, row(mla_kv_norm_g[l]), wq, wqr, wk, wvt, bsz, lp)
        o_mla = _mla_attention(qc.reshape(bsz, lp, -1), kc.reshape(bsz, lp, -1), vt_mla)
        h = _out_ffn(h, y_ssd, o_sb, o_mla, row(sb_norm_g[l]), row(mla_norm_g[l]),
                     w_out[l].astype(BF16), row(norm_ffn_g[l]),
                     _interleave_ff(ffn_w_up[l]).astype(BF16), _interleave_ff(ffn_conv_w[l]),
                     row(_interleave_ff(ffn_conv_b[l])), ffn_w_down[l].astype(BF16))

    return _final_norm(h, row(final_norm_g), seq)
```

```python
import functools
import math

import jax
import jax.numpy as jnp
from jax import lax
from jax.experimental import pallas as pl
from jax.experimental.pallas import tpu as pltpu

F32 = jnp.float32
BF16 = jnp.bfloat16

D_MODEL = 1024
N_META = 16
EPS = 1e-6
SSD_HEADS = 8
SSD_HEAD_DIM = 64
SSD_WIDTH = 512
SSD_STATE = 128
SSD_XBC = 1024
SSD_CHUNK = 128
SB_HEADS = 4
SB_HEAD_DIM = 64
SB_WIDTH = 256
MLA_HEADS = 4
MLA_V = 64
MLA_NOPE = 64
MLA_ROPE = 32
MLA_Q_RANK = 192
MLA_KV_RANK = 128
ROPE_BASE = 10000.0
D_FF = 2816
FF_CHUNK = 256

LANES = 128
ATT_SUB = 256
ATT_TQ = 256
SEQ_ALIGN = ATT_TQ
VMEM_LIMIT = 56 * 1024 * 1024

U_XBC, U_Z, U_QSB, U_KSB, U_VSB, U_QA, U_CKV, U_KR1, U_KR2, U_END = (
    0, 1024, 1536, 1792, 2048, 2304, 2560, 2688, 2816, 2944)

NT_DIMS = (((1,), (1,)), ((), ()))
LOG2E = 1.4426950408889634


def _pick(n, candidates):
    for c in candidates:
        if n % c == 0:
            return c
    raise ValueError(f"no tile in {candidates} divides {n}")


def _softplus(x):
    return jnp.maximum(x, 0.0) + jnp.log(1.0 + jnp.exp2(jnp.abs(x) * (-LOG2E)))


def _silu(x):
    return x * jax.nn.sigmoid(x)


def _rms(x, g, n):
    ms = jnp.sum(x * x, axis=-1, keepdims=True) * (1.0 / n)
    return (x * lax.rsqrt(ms + EPS)) * g


def _params(*sem):
    return pltpu.CompilerParams(dimension_semantics=sem, vmem_limit_bytes=VMEM_LIMIT)


def _in_proj_kernel(x_ref, g_ref, w_ref, u_ref, dt_ref):
    xn = _rms(x_ref[...], g_ref[...], D_MODEL).astype(BF16)
    for n0 in range(0, U_END, 512):
        n1 = min(n0 + 512, U_END)
        u_ref[:, n0:n1] = jnp.dot(xn, w_ref[:, n0:n1], preferred_element_type=F32).astype(BF16)
    dt_ref[...] = jnp.dot(xn, w_ref[:, U_END:U_END + SSD_WIDTH], preferred_element_type=F32)


def _in_proj(h2d, g, w):
    rows = h2d.shape[0]
    tm = _pick(rows, (512, 256))
    return pl.pallas_call(
        _in_proj_kernel,
        grid=(rows // tm,),
        in_specs=[
            pl.BlockSpec((tm, D_MODEL), lambda i: (i, 0)),
            pl.BlockSpec((1, D_MODEL), lambda i: (0, 0)),
            pl.BlockSpec(w.shape, lambda i: (0, 0), pipeline_mode=pl.Buffered(1)),
        ],
        out_specs=[
            pl.BlockSpec((tm, U_END), lambda i: (i, 0)),
            pl.BlockSpec((tm, SSD_WIDTH), lambda i: (i, 0)),
        ],
        out_shape=[
            jax.ShapeDtypeStruct((rows, U_END), BF16),
            jax.ShapeDtypeStruct((rows, SSD_WIDTH), F32),
        ],
        compiler_params=_params("parallel"),
        name="in_proj",
    )(h2d, g, w)


def _cumsum_rows(x):
    n = x.shape[0]
    row = lax.broadcasted_iota(jnp.int32, x.shape, 0)
    s = 1
    while s < n:
        x = x + jnp.where(row >= s, pltpu.roll(x, s, 0), 0.0)
        s *= 2
    return x


def _ssd_kernel(z_ref, xbc_ref, dt_ref, cw_ref, cb_ref, dtb_ref, alog_ref, d_ref, g_ref,
                y_ref, ext_ref, st_ref):
    T = SSD_CHUNK

    @pl.when(pl.program_id(1) == 0)
    def _():
        ext_ref[0:8, :] = jnp.zeros((8, SSD_XBC), F32)
        st_ref[...] = jnp.zeros(st_ref.shape, F32)

    ext_ref[8:8 + T, :] = xbc_ref[0].astype(F32)
    cw = cw_ref[...]
    conv = cb_ref[...] + ext_ref[8:8 + T, :] * cw[3:4]
    for k in range(3):
        conv = conv + ext_ref[5 + k:5 + k + T, :] * cw[k:k + 1]
    ext_ref[0:8, :] = ext_ref[T:T + 8, :]
    xc = _silu(conv)
    xs = xc[:, :SSD_WIDTH]
    bm = xc[:, SSD_WIDTH:SSD_WIDTH + 2 * SSD_STATE]
    cm = xc[:, SSD_WIDTH + 2 * SSD_STATE:]

    dt = _softplus(dt_ref[0] + dtb_ref[...])
    acs = _cumsum_rows(dt * (-jnp.exp(alog_ref[...])))
    acs_last = acs[T - 1:T, :]
    x_dt = xs * dt
    xb = x_dt.astype(BF16)
    xd = (x_dt * jnp.exp(acs_last - acs)).astype(BF16)
    eacs = jnp.exp(acs)
    acs_t = acs.T

    tril = (lax.broadcasted_iota(jnp.int32, (T, T), 0) >= lax.broadcasted_iota(jnp.int32, (T, T), 1))
    low_half = lax.broadcasted_iota(jnp.int32, (T, LANES), 1) < SSD_HEAD_DIM
    y_groups = []
    for g in range(2):
        bg = bm[:, g * SSD_STATE:(g + 1) * SSD_STATE]
        cg = cm[:, g * SSD_STATE:(g + 1) * SSD_STATE].astype(BF16)
        cb = lax.dot_general(cg, bg.astype(BF16), NT_DIMS, preferred_element_type=F32)
        pair_out = []
        for pr in range(2):
            lane0 = g * 256 + pr * LANES
            xp = xb[:, lane0:lane0 + LANES]
            res = []
            for hh in range(2):
                hl = lane0 + hh * SSD_HEAD_DIM
                seg = acs[:, hl:hl + 1] - acs_t[hl:hl + 1, :]
                decay = jnp.where(tril, jnp.exp(seg), 0.0)
                res.append(jnp.dot((cb * decay).astype(BF16), xp, preferred_element_type=F32))
            pair_out.append(jnp.where(low_half, res[0], res[1]))
        y_diag = jnp.concatenate(pair_out, axis=1)
        hg = st_ref[g]
        y_off = jnp.dot(cg, hg.astype(BF16), preferred_element_type=F32) * eacs[:, g * 256:(g + 1) * 256]
        st_ref[g] = hg * jnp.exp(acs_last[:, g * 256:(g + 1) * 256]) + jnp.dot(
            bg.T.astype(BF16), xd[:, g * 256:(g + 1) * 256], preferred_element_type=F32)
        y_groups.append(y_diag + y_off)
    y = jnp.concatenate(y_groups, axis=1) + xs * d_ref[...]
    y = y * _silu(z_ref[0].astype(F32))
    y_ref[0] = _rms(y, g_ref[...], SSD_WIDTH).astype(BF16)


def _ssd(u3, dtx3, cw, cb, dtb, alog, dskip, g):
    bsz, lp, _ = u3.shape
    T = SSD_CHUNK
    vec = lambda n: pl.BlockSpec((1, n), lambda b, c: (0, 0))
    return pl.pallas_call(
        _ssd_kernel,
        grid=(bsz, lp // T),
        in_specs=[
            pl.BlockSpec((1, T, SSD_WIDTH), lambda b, c: (b, c, U_Z // SSD_WIDTH)),
            pl.BlockSpec((1, T, SSD_XBC), lambda b, c: (b, c, U_XBC // SSD_XBC)),
            pl.BlockSpec((1, T, SSD_WIDTH), lambda b, c: (b, c, 0)),
            pl.BlockSpec((4, SSD_XBC), lambda b, c: (0, 0)),
            vec(SSD_XBC), vec(SSD_WIDTH), vec(SSD_WIDTH), vec(SSD_WIDTH), vec(SSD_WIDTH),
        ],
        out_specs=pl.BlockSpec((1, T, SSD_WIDTH), lambda b, c: (b, c, 0)),
        out_shape=jax.ShapeDtypeStruct((bsz, lp, SSD_WIDTH), BF16),
        scratch_shapes=[pltpu.VMEM((T + 8, SSD_XBC), F32), pltpu.VMEM((2, SSD_STATE, 256), F32)],
        compiler_params=_params("parallel", "arbitrary"),
        name="ssd_scan",
    )(u3, u3, dtx3, cw, cb, dtb, alog, dskip, g)


def _key_start(qi, n, tq, lp):
    blk = (tq // ATT_SUB) * (qi + 1) - 1 - n
    return pl.multiple_of(jnp.clip(blk, 0, lp // ATT_SUB - 1) * ATT_SUB, ATT_SUB)


def _causal_masks(tq, strict):
    key_i = lax.broadcasted_iota(jnp.int32, (ATT_SUB, tq), 0)
    qry_i = lax.broadcasted_iota(jnp.int32, (ATT_SUB, tq), 1)
    nd = tq // ATT_SUB
    offs = [(nd - 1 - n) * ATT_SUB for n in range(nd)]
    return [(key_i + o < qry_i) if strict else (key_i + o <= qry_i) for o in offs]


def _head_queries(q, heads):
    lane = lax.broadcasted_iota(jnp.int32, (q.shape[0], LANES), 1)
    out = []
    for h in range(heads):
        pair = q[:, (h // 2) * LANES:(h // 2 + 1) * LANES].astype(F32)
        lo = 64 * (h % 2)
        out.append(jnp.where((lane >= lo) & (lane < lo + 64), pair, 0.0).astype(BF16))
    return out


def _pipeline3(nblk, ndiag, stage1, stage2, stage3):
    stage1(0, 0)
    stage2(0, 0)
    stage1(1, 1)
    stage3(0, 0, True)
    if ndiag > 1:
        stage2(1, 1)
    else:
        stage2(1, None, nblk >= 2)
    stage1(2, 0)

    def step(n, par):
        stage1(n + 2, par)
        stage3(n, par, False)
        stage2(1 - par, None)

    steady = jnp.maximum(nblk - 3, 0)

    def body(i, c):
        step(1 + 2 * i, 1)
        step(2 + 2 * i, 0)
        return c

    lax.fori_loop(0, steady // 2, body, 0)

    @pl.when(steady % 2 == 1)
    def _():
        step(nblk - 3, 1)

    @pl.when(nblk >= 3)
    def _():
        par = (nblk - 2) % 2
        stage3(nblk - 2, par, False)
        stage2(1 - par, None)
        stage3(nblk - 1, 1 - par, False)

    @pl.when(nblk == 2)
    def _():
        stage3(1, 1, False)


def _sb_kernel(q_ref, k_ref, vt_ref, lt_ref, o_ref, s_scr, w_scr, tot_scr, acc_scr, r_scr, *, tq, lp):
    qi = pl.program_id(1)
    ndiag = tq // ATT_SUB
    nblk = ndiag * (qi + 1)
    qms = _head_queries(q_ref[0], SB_HEADS)
    later_t = lt_ref[...]
    masks = _causal_masks(tq, strict=True)
    pair = lambda h: slice((h // 2) * LANES, (h // 2 + 1) * LANES)
    rows = lambda h: slice(h * SB_HEAD_DIM, (h + 1) * SB_HEAD_DIM)

    def stage1(n, par):
        ks = _key_start(qi, n, tq, lp)
        for h in range(SB_HEADS):
            s_scr[par, h] = lax.dot_general(k_ref[0, pl.ds(ks, ATT_SUB), pair(h)], qms[h], NT_DIMS,
                                            preferred_element_type=F32)

    def stage2(par, diag, exists=None):
        for h in range(SB_HEADS):
            s = s_scr[par, h]
            sp = _softplus(s)
            if diag is not None:
                sp = jnp.where(masks[diag], sp, 0.0)
            loc = jnp.dot(later_t, sp.astype(BF16), preferred_element_type=F32)
            w = jnp.exp(s - loc)
            if diag is not None:
                w = jnp.where(masks[diag], w, 0.0)
            w_scr[par, h] = w.astype(BF16)
            tot_scr[par, h] = loc[0:1, :]

    def stage3(n, par, first):
        ks = _key_start(qi, n, tq, lp)
        for h in range(SB_HEADS):
            pv = jnp.dot(vt_ref[0, rows(h), pl.ds(ks, ATT_SUB)], w_scr[par, h], preferred_element_type=F32)
            if first:
                acc_scr[rows(h), :] = pv
                r_scr[h] = -tot_scr[par, h]
            else:
                r = r_scr[h]
                acc_scr[rows(h), :] = acc_scr[rows(h), :] + jnp.exp(r) * pv
                r_scr[h] = r - tot_scr[par, h]

    _pipeline3(nblk, ndiag, stage1, stage2, stage3)
    o_ref[0] = acc_scr[...].T.astype(BF16)


def _sb_attention(u3, vt, later_t):
    bsz, lp, _ = u3.shape
    tq = ATT_TQ
    return pl.pallas_call(
        functools.partial(_sb_kernel, tq=tq, lp=lp),
        grid=(bsz, pl.cdiv(lp, tq)),
        in_specs=[
            pl.BlockSpec((1, tq, SB_WIDTH), lambda b, i: (b, i, U_QSB // SB_WIDTH)),
            pl.BlockSpec((1, lp, SB_WIDTH), lambda b, i: (b, 0, U_KSB // SB_WIDTH)),
            pl.BlockSpec((1, SB_WIDTH, lp), lambda b, i: (b, 0, 0)),
            pl.BlockSpec((ATT_SUB, ATT_SUB), lambda b, i: (0, 0)),
        ],
        out_specs=pl.BlockSpec((1, tq, SB_WIDTH), lambda b, i: (b, i, 0)),
        out_shape=jax.ShapeDtypeStruct((bsz, lp, SB_WIDTH), BF16),
        scratch_shapes=[pltpu.VMEM((2, SB_HEADS, ATT_SUB, tq), F32), pltpu.VMEM((2, SB_HEADS, ATT_SUB, tq), BF16),
                        pltpu.VMEM((2, SB_HEADS, 1, tq), F32), pltpu.VMEM((SB_WIDTH, tq), F32),
                        pltpu.VMEM((SB_HEADS, 1, tq), F32)],
        compiler_params=_params("parallel", "arbitrary"),
        name="sb_attention",
    )(u3, u3, vt, later_t)


def _mla_proj_kernel(qa_ref, ckv_ref, kr1_ref, kr2_ref, cos_ref, sin_ref, gq_ref, gkv_ref,
                     wq_ref, wqr_ref, wk_ref, wvt_ref, qc_ref, kc_ref, vt_ref):
    cos = cos_ref[...]
    sin = sin_ref[...]
    cos4 = jnp.concatenate([cos] * MLA_HEADS, axis=1)
    sin4 = jnp.concatenate([sin] * MLA_HEADS, axis=1)
    qn = _rms(qa_ref[...].astype(F32), gq_ref[...], MLA_Q_RANK).astype(BF16)
    q1 = jnp.dot(qn, wq_ref[...], preferred_element_type=F32)
    q2 = jnp.dot(qn, wqr_ref[...], preferred_element_type=F32)
    scale = (MLA_NOPE + MLA_ROPE) ** -0.5 * LOG2E
    qc_ref[...] = ((q1 * cos4 + q2 * sin4) * scale).astype(BF16)
    cn = _rms(ckv_ref[...].astype(F32), gkv_ref[...], MLA_KV_RANK).astype(BF16)
    kn = jnp.dot(cn, wk_ref[...], preferred_element_type=F32)
    krope = kr1_ref[...].astype(F32) * cos + kr2_ref[...].astype(F32) * sin
    kc_ref[...] = (kn + jnp.concatenate([krope] * MLA_HEADS, axis=1)).astype(BF16)
    vt_ref[0] = lax.dot_general(wvt_ref[...], cn, NT_DIMS, preferred_element_type=F32).astype(BF16)


def _mla_proj(u2d, cosx, sinx, gq, gkv, wq, wqr, wk, wvt, bsz, lp):
    rows = u2d.shape[0]
    tm = _pick(lp, (768, 512, 256))
    nper = lp // tm
    const = lambda a: pl.BlockSpec(a.shape, lambda i: (0, 0))
    return pl.pallas_call(
        _mla_proj_kernel,
        grid=(rows // tm,),
        in_specs=[
            pl.BlockSpec((tm, 256), lambda i: (i, U_QA // 256)),
            pl.BlockSpec((tm, LANES), lambda i: (i, U_CKV // LANES)),
            pl.BlockSpec((tm, LANES), lambda i: (i, U_KR1 // LANES)),
            pl.BlockSpec((tm, LANES), lambda i: (i, U_KR2 // LANES)),
            pl.BlockSpec((tm, LANES), lambda i: (i % nper, 0)),
            pl.BlockSpec((tm, LANES), lambda i: (i % nper, 0)),
            const(gq), const(gkv), const(wq), const(wqr), const(wk), const(wvt),
        ],
        out_specs=[
            pl.BlockSpec((tm, MLA_HEADS * LANES), lambda i: (i, 0)),
            pl.BlockSpec((tm, MLA_HEADS * LANES), lambda i: (i, 0)),
            pl.BlockSpec((1, 256, tm), lambda i: (i // nper, 0, i % nper)),
        ],
        out_shape=[
            jax.ShapeDtypeStruct((rows, MLA_HEADS * LANES), BF16),
            jax.ShapeDtypeStruct((rows, MLA_HEADS * LANES), BF16),
            jax.ShapeDtypeStruct((bsz, 256, lp), BF16),
        ],
        compiler_params=_params("parallel"),
        name="mla_proj",
    )(u2d, u2d, u2d, u2d, cosx, sinx, gq, gkv, wq, wqr, wk, wvt)


def _mla_kernel(q_ref, k_ref, vt_ref, o_ref, s_scr, p_scr, alpha_scr, acc_scr, m_scr, l_scr, *, tq, lp):
    qi = pl.program_id(1)
    ndiag = tq // ATT_SUB
    nblk = ndiag * (qi + 1)
    qs = [q_ref[0, :, h * LANES:(h + 1) * LANES] for h in range(MLA_HEADS)]
    masks = _causal_masks(tq, strict=False)
    rows = lambda h: slice(h * MLA_V, (h + 1) * MLA_V)

    def stage1(n, par):
        ks = _key_start(qi, n, tq, lp)
        for h in range(MLA_HEADS):
            s_scr[par, h] = lax.dot_general(k_ref[0, pl.ds(ks, ATT_SUB), h * LANES:(h + 1) * LANES], qs[h],
                                            NT_DIMS, preferred_element_type=F32)

    def stage2(par, diag, exists=None):
        first = diag == 0
        for h in range(MLA_HEADS):
            s = s_scr[par, h]
            if diag is not None:
                s = jnp.where(masks[diag], s, -1e30)
            if first:
                m_new = jnp.max(s, axis=0, keepdims=True)
            else:
                m_old = m_scr[h]
                m_new = jnp.maximum(m_old, jnp.max(s, axis=0, keepdims=True))
            p = jnp.exp2(s - m_new)
            p_scr[par, h] = p.astype(BF16)
            l_new = jnp.sum(p, axis=0, keepdims=True)
            if not first:
                alpha = jnp.exp2(m_old - m_new)
                alpha_scr[par, h] = alpha
                l_new = alpha * l_scr[h] + l_new
                if exists is not None:
                    l_new = jnp.where(exists, l_new, l_scr[h])
                    m_new = jnp.where(exists, m_new, m_old)
            l_scr[h] = l_new
            m_scr[h] = m_new

    def stage3(n, par, first):
        ks = _key_start(qi, n, tq, lp)
        for h in range(MLA_HEADS):
            pv = jnp.dot(vt_ref[0, rows(h), pl.ds(ks, ATT_SUB)], p_scr[par, h], preferred_element_type=F32)
            if first:
                acc_scr[rows(h), :] = pv
            else:
                acc_scr[rows(h), :] = alpha_scr[par, h] * acc_scr[rows(h), :] + pv

    _pipeline3(nblk, ndiag, stage1, stage2, stage3)
    for h in range(MLA_HEADS):
        acc_scr[rows(h), :] = acc_scr[rows(h), :] / l_scr[h]
    o_ref[0] = acc_scr[...].T.astype(BF16)


def _mla_attention(qc3, kc3, vt):
    bsz, lp, width = qc3.shape
    tq = ATT_TQ
    out_w = MLA_HEADS * MLA_V
    return pl.pallas_call(
        functools.partial(_mla_kernel, tq=tq, lp=lp),
        grid=(bsz, pl.cdiv(lp, tq)),
        in_specs=[
            pl.BlockSpec((1, tq, width), lambda b, i: (b, i, 0)),
            pl.BlockSpec((1, lp, width), lambda b, i: (b, 0, 0)),
            pl.BlockSpec((1, out_w, lp), lambda b, i: (b, 0, 0)),
        ],
        out_specs=pl.BlockSpec((1, tq, out_w), lambda b, i: (b, i, 0)),
        out_shape=jax.ShapeDtypeStruct((bsz, lp, out_w), BF16),
        scratch_shapes=[pltpu.VMEM((2, MLA_HEADS, ATT_SUB, tq), F32), pltpu.VMEM((2, MLA_HEADS, ATT_SUB, tq), BF16),
                        pltpu.VMEM((2, MLA_HEADS, 1, tq), F32), pltpu.VMEM((out_w, tq), F32),
                        pltpu.VMEM((MLA_HEADS, 1, tq), F32), pltpu.VMEM((MLA_HEADS, 1, tq), F32)],
        compiler_params=_params("parallel", "arbitrary"),
        name="mla_attention",
    )(qc3, kc3, vt)


def _out_ffn_kernel(h_ref, yssd_ref, osb_ref, omla_ref, gsb_ref, gmla_ref, wout_ref, gffn_ref,
                    wup_ref, cw_ref, cb_ref, wdown_ref, o_ref, halo_ref, ext_ref, act_ref, *, tm):
    @pl.when(pl.program_id(1) == 0)
    def _():
        halo_ref[...] = jnp.zeros(halo_ref.shape, F32)

    ysb = _rms(osb_ref[0].astype(F32), gsb_ref[...], SB_WIDTH).astype(BF16)
    ymla = _rms(omla_ref[0].astype(F32), gmla_ref[...], 256).astype(BF16)
    mix = jnp.dot(yssd_ref[0], wout_ref[0:512, :], preferred_element_type=F32)
    mix = mix + jnp.dot(ysb, wout_ref[512:768, :], preferred_element_type=F32)
    mix = mix + jnp.dot(ymla, wout_ref[768:1024, :], preferred_element_type=F32)
    h1 = h_ref[0] + mix
    o_ref[0] = h1
    xn = _rms(h1, gffn_ref[...], D_MODEL).astype(BF16)

    w2 = 2 * FF_CHUNK
    for c in range(D_FF // FF_CHUNK):
        cols = slice(c * w2, (c + 1) * w2)
        ext_ref[0:8, :] = halo_ref[:, cols]
        ext_ref[8:8 + tm, :] = jnp.dot(xn, wup_ref[:, cols], preferred_element_type=F32)
        halo_ref[:, cols] = ext_ref[tm:tm + 8, :]
        cw = cw_ref[:, cols]
        conv = cb_ref[:, cols] + ext_ref[8:8 + tm, :] * cw[2:3]
        conv = conv + ext_ref[7:7 + tm, :] * cw[1:2]
        conv = conv + ext_ref[6:6 + tm, :] * cw[0:1]
        act = _silu(conv[:, :FF_CHUNK]) * conv[:, FF_CHUNK:]
        act_ref[:, c * FF_CHUNK:(c + 1) * FF_CHUNK] = act.astype(BF16)
    o_ref[0] = o_ref[0] + jnp.dot(act_ref[...], wdown_ref[...], preferred_element_type=F32)


def _out_ffn(h3, yssd, osb, omla, gsb, gmla, wout, gffn, wup, cw, cb, wdown):
    bsz, lp, _ = h3.shape
    tm = _pick(lp, (768, 512, 256))
    const = lambda a: pl.BlockSpec(a.shape, lambda b, t: (0, 0), pipeline_mode=pl.Buffered(1))
    rowblk = lambda n: pl.BlockSpec((1, tm, n), lambda b, t: (b, t, 0))
    return pl.pallas_call(
        functools.partial(_out_ffn_kernel, tm=tm),
        grid=(bsz, lp // tm),
        in_specs=[rowblk(D_MODEL), rowblk(SSD_WIDTH), rowblk(256), rowblk(256),
                  const(gsb), const(gmla), const(wout), const(gffn),
                  const(wup), const(cw), const(cb), const(wdown)],
        out_specs=rowblk(D_MODEL),
        out_shape=jax.ShapeDtypeStruct(h3.shape, F32),
        scratch_shapes=[pltpu.VMEM((8, 2 * D_FF), F32),
                        pltpu.VMEM((tm + 8, 2 * FF_CHUNK), F32),
                        pltpu.VMEM((tm, D_FF), BF16)],
        compiler_params=_params("parallel", "arbitrary"),
        name="out_ffn",
    )(h3, yssd, osb, omla, gsb, gmla, wout, gffn, wup, cw, cb, wdown)


def _final_kernel(a_ref, t_ref, g_ref, o_ref, *, tm):
    g = g_ref[...]
    o_ref[0, 0:tm - N_META, :] = _rms(a_ref[0, N_META:tm, :], g, D_MODEL)
    o_ref[0, tm - N_META:tm, :] = _rms(t_ref[0], g, D_MODEL)


def _final_norm(h3, g, seq):
    bsz = h3.shape[0]
    tm = _pick(seq, (512, 128))
    return pl.pallas_call(
        functools.partial(_final_kernel, tm=tm),
        grid=(bsz, seq // tm),
        in_specs=[
            pl.BlockSpec((1, tm, D_MODEL), lambda b, i: (b, i, 0)),
            pl.BlockSpec((1, N_META, D_MODEL), lambda b, i: (b, (i + 1) * (tm // N_META), 0)),
            pl.BlockSpec((1, D_MODEL), lambda b, i: (0, 0)),
        ],
        out_specs=pl.BlockSpec((1, tm, D_MODEL), lambda b, i: (b, i, 0)),
        out_shape=jax.ShapeDtypeStruct((bsz, seq, D_MODEL), F32),
        compiler_params=_params("parallel", "parallel"),
        name="final_norm",
    )(h3, h3, g)


def _rot_cols(w):
    half = w.shape[1] // 2
    return jnp.concatenate([-w[:, half:], w[:, :half]], axis=1)


def _prep_in_proj(w_in):
    zeros = lambda n: jnp.zeros((D_MODEL, n), F32)
    cuts = (512, 1536, 1544, 1800, 2056, 2312, 2504, 2632)
    z, xbc, dt, q_sb, k_sb, v_sb, q_a, c_kv, k_r = jnp.split(w_in, cuts, axis=1)
    kr1 = jnp.concatenate([zeros(MLA_NOPE), k_r, zeros(LANES - MLA_NOPE - MLA_ROPE)], axis=1)
    kr2 = jnp.concatenate([zeros(MLA_NOPE), _rot_cols(k_r), zeros(LANES - MLA_NOPE - MLA_ROPE)], axis=1)
    w = jnp.concatenate([xbc, z, q_sb * (64 ** -0.5), k_sb, v_sb, q_a, zeros(256 - MLA_Q_RANK), c_kv,
                         kr1, kr2, jnp.repeat(dt, SSD_HEAD_DIM, axis=1)], axis=1)
    return w.astype(BF16)


def _prep_mla(w_uq, w_ukv):
    qh = w_uq.reshape(MLA_Q_RANK, MLA_HEADS, MLA_NOPE + MLA_ROPE)
    pad = jnp.zeros((MLA_Q_RANK, MLA_HEADS, LANES - MLA_NOPE - MLA_ROPE), F32)
    nope0 = jnp.zeros((MLA_Q_RANK, MLA_HEADS, MLA_NOPE), F32)
    rope = qh[..., MLA_NOPE:]
    rope_rot = jnp.concatenate([-rope[..., MLA_ROPE // 2:], rope[..., :MLA_ROPE // 2]], axis=-1)
    wq = jnp.concatenate([qh, pad], axis=-1).reshape(MLA_Q_RANK, MLA_HEADS * LANES)
    wqr = jnp.concatenate([nope0, rope_rot, pad], axis=-1).reshape(MLA_Q_RANK, MLA_HEADS * LANES)
    rowpad = jnp.zeros((256 - MLA_Q_RANK, MLA_HEADS * LANES), F32)
    wq = jnp.concatenate([wq, rowpad], axis=0).astype(BF16)
    wqr = jnp.concatenate([wqr, rowpad], axis=0).astype(BF16)
    kvh = w_ukv.reshape(MLA_KV_RANK, MLA_HEADS, MLA_NOPE + 64)
    kpad = jnp.zeros((MLA_KV_RANK, MLA_HEADS, LANES - MLA_NOPE), F32)
    wk = jnp.concatenate([kvh[..., :MLA_NOPE], kpad], axis=-1).reshape(MLA_KV_RANK, MLA_HEADS * LANES)
    wvt = kvh[..., MLA_NOPE:].reshape(MLA_KV_RANK, MLA_HEADS * 64).T
    return wq, wqr, wk.astype(BF16), wvt.astype(BF16)


def _interleave_ff(a):
    lead = a.shape[:-1]
    g = a[..., :D_FF].reshape(lead + (D_FF // FF_CHUNK, FF_CHUNK))
    v = a[..., D_FF:].reshape(lead + (D_FF // FF_CHUNK, FF_CHUNK))
    return jnp.concatenate([g, v], axis=-1).reshape(lead + (2 * D_FF,))


def _rope_tables(lp):
    pos = jnp.arange(lp, dtype=F32)
    inv = 1.0 / (ROPE_BASE ** (jnp.arange(0, MLA_ROPE, 2, dtype=F32) / MLA_ROPE))
    ang = pos[:, None] * inv[None, :]
    ang = jnp.concatenate([ang, ang], axis=-1)
    ones = jnp.ones((lp, MLA_NOPE), F32)
    tail = LANES - MLA_NOPE - MLA_ROPE
    cosx = jnp.concatenate([ones, jnp.cos(ang), jnp.ones((lp, tail), F32)], axis=1)
    sinx = jnp.concatenate([0 * ones, jnp.sin(ang), jnp.zeros((lp, tail), F32)], axis=1)
    return cosx, sinx


def kernel(x, meta_tokens, norm_mix_g, w_in, ssd_conv_w, ssd_conv_b, ssd_dt_bias, ssd_a_log, ssd_d, ssd_norm_g, sb_norm_g, mla_q_norm_g, mla_kv_norm_g, mla_w_uq, mla_w_ukv, mla_norm_g, w_out, norm_ffn_g, ffn_w_up, ffn_conv_w, ffn_conv_b, ffn_w_down, final_norm_g):
    bsz, seq, _ = x.shape
    length = N_META + seq
    lp = -(-length // SEQ_ALIGN) * SEQ_ALIGN
    depth = w_in.shape[0]

    meta = jnp.broadcast_to(meta_tokens[None].astype(x.dtype), (bsz, N_META, D_MODEL))
    h = jnp.concatenate([meta, x, jnp.zeros((bsz, lp - length, D_MODEL), x.dtype)], axis=1)

    cosx, sinx = _rope_tables(lp)
    later_t = (lax.broadcasted_iota(jnp.int32, (ATT_SUB, ATT_SUB), 1) >=
               lax.broadcasted_iota(jnp.int32, (ATT_SUB, ATT_SUB), 0)).astype(BF16)
    row = lambda v: v.reshape(1, -1)
    rep = lambda v: jnp.repeat(v, SSD_HEAD_DIM).reshape(1, -1)

    for l in range(depth):
        u, dtx = _in_proj(h.reshape(bsz * lp, D_MODEL), row(norm_mix_g[l]), _prep_in_proj(w_in[l]))
        u3 = u.reshape(bsz, lp, U_END)
        y_ssd = _ssd(u3, dtx.reshape(bsz, lp, SSD_WIDTH), ssd_conv_w[l], row(ssd_conv_b[l]),
                     rep(ssd_dt_bias[l]), rep(ssd_a_log[l]), rep(ssd_d[l]), row(ssd_norm_g[l]))
        vt_sb = jnp.swapaxes(u3[:, :, U_VSB:U_VSB + SB_WIDTH], 1, 2)
        o_sb = _sb_attention(u3, vt_sb, later_t)
        wq, wqr, wk, wvt = _prep_mla(mla_w_uq[l], mla_w_ukv[l])
        gq = jnp.concatenate([mla_q_norm_g[l], jnp.zeros((256 - MLA_Q_RANK,), F32)]).reshape(1, -1)
        qc, kc, vt_mla = _mla_proj(u, cosx, sinx, gq, row(mla_kv_norm_g[l]), wq, wqr, wk, wvt, bsz, lp)
        o_mla = _mla_attention(qc.reshape(bsz, lp, -1), kc.reshape(bsz, lp, -1), vt_mla)
        h = _out_ffn(h, y_ssd, o_sb, o_mla, row(sb_norm_g[l]), row(mla_norm_g[l]),
                     w_out[l].astype(BF16), row(norm_ffn_g[l]),
                     _interleave_ff(ffn_w_up[l]).astype(BF16), _interleave_ff(ffn_conv_w[l]),
                     row(_interleave_ff(ffn_conv_b[l])), ffn_w_down[l].astype(BF16))

    return _final_norm(h, row(final_norm_g), seq)
```

```python
import functools
import math

import jax
import jax.numpy as jnp
from jax import lax
from jax.experimental import pallas as pl
from jax.experimental.pallas import tpu as pltpu

F32 = jnp.float32
BF16 = jnp.bfloat16

D_MODEL = 1024
N_META = 16
EPS = 1e-6
SSD_HEADS = 8
SSD_HEAD_DIM = 64
SSD_WIDTH = 512
SSD_STATE = 128
SSD_XBC = 1024
SSD_CHUNK = 128
SB_HEADS = 4
SB_HEAD_DIM = 64
SB_WIDTH = 256
MLA_HEADS = 4
MLA_V = 64
MLA_NOPE = 64
MLA_ROPE = 32
MLA_Q_RANK = 192
MLA_KV_RANK = 128
ROPE_BASE = 10000.0
D_FF = 2816
FF_CHUNK = 256

LANES = 128
ATT_SUB = 256
ATT_TQ = 256
PIPE_UNROLL = 4
SEQ_ALIGN = ATT_TQ
VMEM_LIMIT = 56 * 1024 * 1024

U_XBC, U_Z, U_QSB, U_KSB, U_VSB, U_QA, U_CKV, U_KR1, U_KR2, U_END = (
    0, 1024, 1536, 1792, 2048, 2304, 2560, 2688, 2816, 2944)

NT_DIMS = (((1,), (1,)), ((), ()))
LOG2E = 1.4426950408889634


def _pick(n, candidates):
    for c in candidates:
        if n % c == 0:
            return c
    raise ValueError(f"no tile in {candidates} divides {n}")


def _softplus(x):
    return jnp.maximum(x, 0.0) + jnp.log(1.0 + jnp.exp2(jnp.abs(x) * (-LOG2E)))


def _silu(x):
    return x * jax.nn.sigmoid(x)


def _rms(x, g, n):
    ms = jnp.sum(x * x, axis=-1, keepdims=True) * (1.0 / n)
    return (x * lax.rsqrt(ms + EPS)) * g


def _params(*sem):
    return pltpu.CompilerParams(dimension_semantics=sem, vmem_limit_bytes=VMEM_LIMIT)


def _in_proj_kernel(x_ref, g_ref, w_ref, u_ref, dt_ref):
    xn = _rms(x_ref[...], g_ref[...], D_MODEL).astype(BF16)
    for n0 in range(0, U_END, 512):
        n1 = min(n0 + 512, U_END)
        u_ref[:, n0:n1] = jnp.dot(xn, w_ref[:, n0:n1], preferred_element_type=F32).astype(BF16)
    dt_ref[...] = jnp.dot(xn, w_ref[:, U_END:U_END + SSD_WIDTH], preferred_element_type=F32)


def _in_proj(h2d, g, w):
    rows = h2d.shape[0]
    tm = _pick(rows, (512, 256))
    return pl.pallas_call(
        _in_proj_kernel,
        grid=(rows // tm,),
        in_specs=[
            pl.BlockSpec((tm, D_MODEL), lambda i: (i, 0)),
            pl.BlockSpec((1, D_MODEL), lambda i: (0, 0)),
            pl.BlockSpec(w.shape, lambda i: (0, 0), pipeline_mode=pl.Buffered(1)),
        ],
        out_specs=[
            pl.BlockSpec((tm, U_END), lambda i: (i, 0)),
            pl.BlockSpec((tm, SSD_WIDTH), lambda i: (i, 0)),
        ],
        out_shape=[
            jax.ShapeDtypeStruct((rows, U_END), BF16),
            jax.ShapeDtypeStruct((rows, SSD_WIDTH), F32),
        ],
        compiler_params=_params("parallel"),
        name="in_proj",
    )(h2d, g, w)


def _cumsum_rows(x):
    n = x.shape[0]
    row = lax.broadcasted_iota(jnp.int32, x.shape, 0)
    s = 1
    while s < n:
        x = x + jnp.where(row >= s, pltpu.roll(x, s, 0), 0.0)
        s *= 2
    return x


def _ssd_kernel(z_ref, xbc_ref, dt_ref, cw_ref, cb_ref, dtb_ref, alog_ref, d_ref, g_ref,
                y_ref, ext_ref, st_ref):
    T = SSD_CHUNK

    @pl.when(pl.program_id(1) == 0)
    def _():
        ext_ref[0:8, :] = jnp.zeros((8, SSD_XBC), F32)
        st_ref[...] = jnp.zeros(st_ref.shape, F32)

    ext_ref[8:8 + T, :] = xbc_ref[0].astype(F32)
    cw = cw_ref[...]
    conv = cb_ref[...] + ext_ref[8:8 + T, :] * cw[3:4]
    for k in range(3):
        conv = conv + ext_ref[5 + k:5 + k + T, :] * cw[k:k + 1]
    ext_ref[0:8, :] = ext_ref[T:T + 8, :]
    xc = _silu(conv)
    xs = xc[:, :SSD_WIDTH]
    bm = xc[:, SSD_WIDTH:SSD_WIDTH + 2 * SSD_STATE]
    cm = xc[:, SSD_WIDTH + 2 * SSD_STATE:]

    dt = _softplus(dt_ref[0] + dtb_ref[...])
    acs = _cumsum_rows(dt * (-jnp.exp(alog_ref[...])))
    acs_last = acs[T - 1:T, :]
    x_dt = xs * dt
    xb = x_dt.astype(BF16)
    xd = (x_dt * jnp.exp(acs_last - acs)).astype(BF16)
    eacs = jnp.exp(acs)
    acs_t = acs.T

    tril = (lax.broadcasted_iota(jnp.int32, (T, T), 0) >= lax.broadcasted_iota(jnp.int32, (T, T), 1))
    low_half = lax.broadcasted_iota(jnp.int32, (T, LANES), 1) < SSD_HEAD_DIM
    y_groups = []
    for g in range(2):
        bg = bm[:, g * SSD_STATE:(g + 1) * SSD_STATE]
        cg = cm[:, g * SSD_STATE:(g + 1) * SSD_STATE].astype(BF16)
        cb = lax.dot_general(cg, bg.astype(BF16), NT_DIMS, preferred_element_type=F32)
        pair_out = []
        for pr in range(2):
            lane0 = g * 256 + pr * LANES
            xp = xb[:, lane0:lane0 + LANES]
            res = []
            for hh in range(2):
                hl = lane0 + hh * SSD_HEAD_DIM
                seg = acs[:, hl:hl + 1] - acs_t[hl:hl + 1, :]
                decay = jnp.where(tril, jnp.exp(seg), 0.0)
                res.append(jnp.dot((cb * decay).astype(BF16), xp, preferred_element_type=F32))
            pair_out.append(jnp.where(low_half, res[0], res[1]))
        y_diag = jnp.concatenate(pair_out, axis=1)
        hg = st_ref[g]
        y_off = jnp.dot(cg, hg.astype(BF16), preferred_element_type=F32) * eacs[:, g * 256:(g + 1) * 256]
        st_ref[g] = hg * jnp.exp(acs_last[:, g * 256:(g + 1) * 256]) + jnp.dot(
            bg.T.astype(BF16), xd[:, g * 256:(g + 1) * 256], preferred_element_type=F32)
        y_groups.append(y_diag + y_off)
    y = jnp.concatenate(y_groups, axis=1) + xs * d_ref[...]
    y = y * _silu(z_ref[0].astype(F32))
    y_ref[0] = _rms(y, g_ref[...], SSD_WIDTH).astype(BF16)


def _ssd(u3, dtx3, cw, cb, dtb, alog, dskip, g):
    bsz, lp, _ = u3.shape
    T = SSD_CHUNK
    vec = lambda n: pl.BlockSpec((1, n), lambda b, c: (0, 0))
    return pl.pallas_call(
        _ssd_kernel,
        grid=(bsz, lp // T),
        in_specs=[
            pl.BlockSpec((1, T, SSD_WIDTH), lambda b, c: (b, c, U_Z // SSD_WIDTH)),
            pl.BlockSpec((1, T, SSD_XBC), lambda b, c: (b, c, U_XBC // SSD_XBC)),
            pl.BlockSpec((1, T, SSD_WIDTH), lambda b, c: (b, c, 0)),
            pl.BlockSpec((4, SSD_XBC), lambda b, c: (0, 0)),
            vec(SSD_XBC), vec(SSD_WIDTH), vec(SSD_WIDTH), vec(SSD_WIDTH), vec(SSD_WIDTH),
        ],
        out_specs=pl.BlockSpec((1, T, SSD_WIDTH), lambda b, c: (b, c, 0)),
        out_shape=jax.ShapeDtypeStruct((bsz, lp, SSD_WIDTH), BF16),
        scratch_shapes=[pltpu.VMEM((T + 8, SSD_XBC), F32), pltpu.VMEM((2, SSD_STATE, 256), F32)],
        compiler_params=_params("parallel", "arbitrary"),
        name="ssd_scan",
    )(u3, u3, dtx3, cw, cb, dtb, alog, dskip, g)


def _key_start(qi, n, tq, lp):
    blk = (tq // ATT_SUB) * (qi + 1) - 1 - n
    return pl.multiple_of(jnp.clip(blk, 0, lp // ATT_SUB - 1) * ATT_SUB, ATT_SUB)


def _causal_masks(tq, strict):
    key_i = lax.broadcasted_iota(jnp.int32, (ATT_SUB, tq), 0)
    qry_i = lax.broadcasted_iota(jnp.int32, (ATT_SUB, tq), 1)
    nd = tq // ATT_SUB
    offs = [(nd - 1 - n) * ATT_SUB for n in range(nd)]
    return [(key_i + o < qry_i) if strict else (key_i + o <= qry_i) for o in offs]


def _head_queries(q, heads):
    lane = lax.broadcasted_iota(jnp.int32, (q.shape[0], LANES), 1)
    out = []
    for h in range(heads):
        pair = q[:, (h // 2) * LANES:(h // 2 + 1) * LANES].astype(F32)
        lo = 64 * (h % 2)
        out.append(jnp.where((lane >= lo) & (lane < lo + 64), pair, 0.0).astype(BF16))
    return out


def _pipeline3(nblk, ndiag, stage1, stage2, stage3):
    diag1 = 1 if ndiag > 1 else None
    stage1(0, 0, 0)
    stage2(0, 0)
    stage1(1, 1, diag1)
    stage3(0, 0, True)
    if ndiag > 1:
        stage2(1, 1)
    else:
        stage2(1, None, nblk >= 2)
    stage1(2, 0)

    def step(n, par):
        stage1(n + 2, par)
        stage3(n, par, False)
        stage2(1 - par, None)

    def group(n0, count):
        for j in range(count):
            step(n0 + j, 1 - j % 2)

    steady = jnp.maximum(nblk - 3, 0)
    whole = steady // PIPE_UNROLL

    def body(i, c):
        group(1 + PIPE_UNROLL * i, PIPE_UNROLL)
        return c

    lax.fori_loop(0, whole, body, 0)
    rest = steady - whole * PIPE_UNROLL
    size = PIPE_UNROLL // 2
    while size >= 1:
        done = rest - rest % (2 * size)

        @pl.when(rest % (2 * size) >= size)
        def _(size=size, done=done):
            group(1 + whole * PIPE_UNROLL + done, size)

        size //= 2

    @pl.when(nblk >= 3)
    def _():
        par = (nblk - 2) % 2
        stage3(nblk - 2, par, False)
        stage2(1 - par, None)
        stage3(nblk - 1, 1 - par, False)

    @pl.when(nblk == 2)
    def _():
        stage3(1, 1, False)


def _sb_kernel(q_ref, k_ref, vt_ref, lt_ref, o_ref, s_scr, w_scr, tot_scr, acc_scr, r_scr, *, tq, lp):
    qi = pl.program_id(1)
    ndiag = tq // ATT_SUB
    nblk = ndiag * (qi + 1)
    qms = _head_queries(q_ref[0], SB_HEADS)
    later_t = lt_ref[...]
    masks = _causal_masks(tq, strict=True)
    pair = lambda h: slice((h // 2) * LANES, (h // 2 + 1) * LANES)
    rows = lambda h: slice(h * SB_HEAD_DIM, (h + 1) * SB_HEAD_DIM)

    def stage1(n, par, diag=None):
        ks = _key_start(qi, n, tq, lp)
        for h in range(SB_HEADS):
            s_scr[par, h] = lax.dot_general(k_ref[0, pl.ds(ks, ATT_SUB), pair(h)], qms[h], NT_DIMS,
                                            preferred_element_type=F32)

    def stage2(par, diag, exists=None):
        for h in range(SB_HEADS):
            s = s_scr[par, h]
            sp = _softplus(s)
            if diag is not None:
                sp = jnp.where(masks[diag], sp, 0.0)
            loc = jnp.dot(later_t, sp.astype(BF16), preferred_element_type=F32)
            w = jnp.exp(s - loc)
            if diag is not None:
                w = jnp.where(masks[diag], w, 0.0)
            w_scr[par, h] = w.astype(BF16)
            tot_scr[par, h] = loc[0:1, :]

    def stage3(n, par, first):
        ks = _key_start(qi, n, tq, lp)
        for h in range(SB_HEADS):
            pv = jnp.dot(vt_ref[0, rows(h), pl.ds(ks, ATT_SUB)], w_scr[par, h], preferred_element_type=F32)
            if first:
                acc_scr[rows(h), :] = pv
                r_scr[h] = -tot_scr[par, h]
            else:
                r = r_scr[h]
                acc_scr[rows(h), :] = acc_scr[rows(h), :] + jnp.exp(r) * pv
                r_scr[h] = r - tot_scr[par, h]

    _pipeline3(nblk, ndiag, stage1, stage2, stage3)
    o_ref[0] = acc_scr[...].T.astype(BF16)


def _sb_attention(u3, vt, later_t):
    bsz, lp, _ = u3.shape
    tq = ATT_TQ
    return pl.pallas_call(
        functools.partial(_sb_kernel, tq=tq, lp=lp),
        grid=(bsz, pl.cdiv(lp, tq)),
        in_specs=[
            pl.BlockSpec((1, tq, SB_WIDTH), lambda b, i: (b, i, U_QSB // SB_WIDTH)),
            pl.BlockSpec((1, lp, SB_WIDTH), lambda b, i: (b, 0, U_KSB // SB_WIDTH)),
            pl.BlockSpec((1, SB_WIDTH, lp), lambda b, i: (b, 0, 0)),
            pl.BlockSpec((ATT_SUB, ATT_SUB), lambda b, i: (0, 0)),
        ],
        out_specs=pl.BlockSpec((1, tq, SB_WIDTH), lambda b, i: (b, i, 0)),
        out_shape=jax.ShapeDtypeStruct((bsz, lp, SB_WIDTH), BF16),
        scratch_shapes=[pltpu.VMEM((2, SB_HEADS, ATT_SUB, tq), F32), pltpu.VMEM((2, SB_HEADS, ATT_SUB, tq), BF16),
                        pltpu.VMEM((2, SB_HEADS, 1, tq), F32), pltpu.VMEM((SB_WIDTH, tq), F32),
                        pltpu.VMEM((SB_HEADS, 1, tq), F32)],
        compiler_params=_params("parallel", "arbitrary"),
        name="sb_attention",
    )(u3, u3, vt, later_t)


def _mla_proj_kernel(qa_ref, ckv_ref, kr1_ref, kr2_ref, cos_ref, sin_ref, gq_ref, gkv_ref,
                     wq_ref, wqr_ref, wk_ref, wvt_ref, qc_ref, kc_ref, vt_ref):
    cos = cos_ref[...]
    sin = sin_ref[...]
    cos4 = jnp.concatenate([cos] * MLA_HEADS, axis=1)
    sin4 = jnp.concatenate([sin] * MLA_HEADS, axis=1)
    qn = _rms(qa_ref[...].astype(F32), gq_ref[...], MLA_Q_RANK).astype(BF16)
    q1 = jnp.dot(qn, wq_ref[...], preferred_element_type=F32)
    q2 = jnp.dot(qn, wqr_ref[...], preferred_element_type=F32)
    scale = (MLA_NOPE + MLA_ROPE) ** -0.5 * LOG2E
    qc_ref[...] = ((q1 * cos4 + q2 * sin4) * scale).astype(BF16)
    cn = _rms(ckv_ref[...].astype(F32), gkv_ref[...], MLA_KV_RANK).astype(BF16)
    kn = jnp.dot(cn, wk_ref[...], preferred_element_type=F32)
    krope = kr1_ref[...].astype(F32) * cos + kr2_ref[...].astype(F32) * sin
    kc_ref[...] = (kn + jnp.concatenate([krope] * MLA_HEADS, axis=1)).astype(BF16)
    vt_ref[0] = lax.dot_general(wvt_ref[...], cn, NT_DIMS, preferred_element_type=F32).astype(BF16)


def _mla_proj(u2d, cosx, sinx, gq, gkv, wq, wqr, wk, wvt, bsz, lp):
    rows = u2d.shape[0]
    tm = _pick(lp, (768, 512, 256))
    nper = lp // tm
    const = lambda a: pl.BlockSpec(a.shape, lambda i: (0, 0))
    return pl.pallas_call(
        _mla_proj_kernel,
        grid=(rows // tm,),
        in_specs=[
            pl.BlockSpec((tm, 256), lambda i: (i, U_QA // 256)),
            pl.BlockSpec((tm, LANES), lambda i: (i, U_CKV // LANES)),
            pl.BlockSpec((tm, LANES), lambda i: (i, U_KR1 // LANES)),
            pl.BlockSpec((tm, LANES), lambda i: (i, U_KR2 // LANES)),
            pl.BlockSpec((tm, LANES), lambda i: (i % nper, 0)),
            pl.BlockSpec((tm, LANES), lambda i: (i % nper, 0)),
            const(gq), const(gkv), const(wq), const(wqr), const(wk), const(wvt),
        ],
        out_specs=[
            pl.BlockSpec((tm, MLA_HEADS * LANES), lambda i: (i, 0)),
            pl.BlockSpec((tm, MLA_HEADS * LANES), lambda i: (i, 0)),
            pl.BlockSpec((1, 256, tm), lambda i: (i // nper, 0, i % nper)),
        ],
        out_shape=[
            jax.ShapeDtypeStruct((rows, MLA_HEADS * LANES), BF16),
            jax.ShapeDtypeStruct((rows, MLA_HEADS * LANES), BF16),
            jax.ShapeDtypeStruct((bsz, 256, lp), BF16),
        ],
        compiler_params=_params("parallel"),
        name="mla_proj",
    )(u2d, u2d, u2d, u2d, cosx, sinx, gq, gkv, wq, wqr, wk, wvt)


def _mla_kernel(q_ref, k_ref, vt_ref, o_ref, s_scr, cmax_scr, p_scr, alpha_scr, acc_scr, m_scr, l_scr,
                *, tq, lp):
    qi = pl.program_id(1)
    ndiag = tq // ATT_SUB
    nblk = ndiag * (qi + 1)
    qs = [q_ref[0, :, h * LANES:(h + 1) * LANES] for h in range(MLA_HEADS)]
    masks = _causal_masks(tq, strict=False)
    rows = lambda h: slice(h * MLA_V, (h + 1) * MLA_V)

    def stage1(n, par, diag=None):
        ks = _key_start(qi, n, tq, lp)
        for h in range(MLA_HEADS):
            s = lax.dot_general(k_ref[0, pl.ds(ks, ATT_SUB), h * LANES:(h + 1) * LANES], qs[h],
                                NT_DIMS, preferred_element_type=F32)
            if diag is not None:
                s = jnp.where(masks[diag], s, -1e30)
            s_scr[par, h] = s
            cmax_scr[par, h] = jnp.max(s, axis=0, keepdims=True)

    def stage2(par, diag, exists=None):
        first = diag == 0
        for h in range(MLA_HEADS):
            if first:
                m_new = cmax_scr[par, h]
            else:
                m_old = m_scr[h]
                m_new = jnp.maximum(m_old, cmax_scr[par, h])
            p = jnp.exp2(s_scr[par, h] - m_new)
            p_scr[par, h] = p.astype(BF16)
            l_new = jnp.sum(p, axis=0, keepdims=True)
            if not first:
                alpha = jnp.exp2(m_old - m_new)
                alpha_scr[par, h] = alpha
                l_new = alpha * l_scr[h] + l_new
                if exists is not None:
                    l_new = jnp.where(exists, l_new, l_scr[h])
                    m_new = jnp.where(exists, m_new, m_old)
            l_scr[h] = l_new
            m_scr[h] = m_new

    def stage3(n, par, first):
        ks = _key_start(qi, n, tq, lp)
        for h in range(MLA_HEADS):
            pv = jnp.dot(vt_ref[0, rows(h), pl.ds(ks, ATT_SUB)], p_scr[par, h], preferred_element_type=F32)
            if first:
                acc_scr[rows(h), :] = pv
            else:
                acc_scr[rows(h), :] = alpha_scr[par, h] * acc_scr[rows(h), :] + pv

    _pipeline3(nblk, ndiag, stage1, stage2, stage3)
    for h in range(MLA_HEADS):
        acc_scr[rows(h), :] = acc_scr[rows(h), :] / l_scr[h]
    o_ref[0] = acc_scr[...].T.astype(BF16)


def _mla_attention(qc3, kc3, vt):
    bsz, lp, width = qc3.shape
    tq = ATT_TQ
    out_w = MLA_HEADS * MLA_V
    return pl.pallas_call(
        functools.partial(_mla_kernel, tq=tq, lp=lp),
        grid=(bsz, pl.cdiv(lp, tq)),
        in_specs=[
            pl.BlockSpec((1, tq, width), lambda b, i: (b, i, 0)),
            pl.BlockSpec((1, lp, width), lambda b, i: (b, 0, 0)),
            pl.BlockSpec((1, out_w, lp), lambda b, i: (b, 0, 0)),
        ],
        out_specs=pl.BlockSpec((1, tq, out_w), lambda b, i: (b, i, 0)),
        out_shape=jax.ShapeDtypeStruct((bsz, lp, out_w), BF16),
        scratch_shapes=[pltpu.VMEM((2, MLA_HEADS, ATT_SUB, tq), F32), pltpu.VMEM((2, MLA_HEADS, 1, tq), F32),
                        pltpu.VMEM((2, MLA_HEADS, ATT_SUB, tq), BF16),
                        pltpu.VMEM((2, MLA_HEADS, 1, tq), F32), pltpu.VMEM((out_w, tq), F32),
                        pltpu.VMEM((MLA_HEADS, 1, tq), F32), pltpu.VMEM((MLA_HEADS, 1, tq), F32)],
        compiler_params=_params("parallel", "arbitrary"),
        name="mla_attention",
    )(qc3, kc3, vt)


def _out_ffn_kernel(h_ref, yssd_ref, osb_ref, omla_ref, gsb_ref, gmla_ref, wout_ref, gffn_ref,
                    wup_ref, cw_ref, cb_ref, wdown_ref, o_ref, halo_ref, ext_ref, act_ref, *, tm):
    @pl.when(pl.program_id(1) == 0)
    def _():
        halo_ref[...] = jnp.zeros(halo_ref.shape, F32)

    ysb = _rms(osb_ref[0].astype(F32), gsb_ref[...], SB_WIDTH).astype(BF16)
    ymla = _rms(omla_ref[0].astype(F32), gmla_ref[...], 256).astype(BF16)
    mix = jnp.dot(yssd_ref[0], wout_ref[0:512, :], preferred_element_type=F32)
    mix = mix + jnp.dot(ysb, wout_ref[512:768, :], preferred_element_type=F32)
    mix = mix + jnp.dot(ymla, wout_ref[768:1024, :], preferred_element_type=F32)
    h1 = h_ref[0] + mix
    o_ref[0] = h1
    xn = _rms(h1, gffn_ref[...], D_MODEL).astype(BF16)

    w2 = 2 * FF_CHUNK
    for c in range(D_FF // FF_CHUNK):
        cols = slice(c * w2, (c + 1) * w2)
        ext_ref[0:8, :] = halo_ref[:, cols]
        ext_ref[8:8 + tm, :] = jnp.dot(xn, wup_ref[:, cols], preferred_element_type=F32)
        halo_ref[:, cols] = ext_ref[tm:tm + 8, :]
        cw = cw_ref[:, cols]
        conv = cb_ref[:, cols] + ext_ref[8:8 + tm, :] * cw[2:3]
        conv = conv + ext_ref[7:7 + tm, :] * cw[1:2]
        conv = conv + ext_ref[6:6 + tm, :] * cw[0:1]
        act = _silu(conv[:, :FF_CHUNK]) * conv[:, FF_CHUNK:]
        act_ref[:, c * FF_CHUNK:(c + 1) * FF_CHUNK] = act.astype(BF16)
    o_ref[0] = o_ref[0] + jnp.dot(act_ref[...], wdown_ref[...], preferred_element_type=F32)


def _out_ffn(h3, yssd, osb, omla, gsb, gmla, wout, gffn, wup, cw, cb, wdown):
    bsz, lp, _ = h3.shape
    tm = _pick(lp, (768, 512, 256))
    const = lambda a: pl.BlockSpec(a.shape, lambda b, t: (0, 0), pipeline_mode=pl.Buffered(1))
    rowblk = lambda n: pl.BlockSpec((1, tm, n), lambda b, t: (b, t, 0))
    return pl.pallas_call(
        functools.partial(_out_ffn_kernel, tm=tm),
        grid=(bsz, lp // tm),
        in_specs=[rowblk(D_MODEL), rowblk(SSD_WIDTH), rowblk(256), rowblk(256),
                  const(gsb), const(gmla), const(wout), const(gffn),
                  const(wup), const(cw), const(cb), const(wdown)],
        out_specs=rowblk(D_MODEL),
        out_shape=jax.ShapeDtypeStruct(h3.shape, F32),
        scratch_shapes=[pltpu.VMEM((8, 2 * D_FF), F32),
                        pltpu.VMEM((tm + 8, 2 * FF_CHUNK), F32),
                        pltpu.VMEM((tm, D_FF), BF16)],
        compiler_params=_params("parallel", "arbitrary"),
        name="out_ffn",
    )(h3, yssd, osb, omla, gsb, gmla, wout, gffn, wup, cw, cb, wdown)


def _final_kernel(a_ref, t_ref, g_ref, o_ref, *, tm):
    g = g_ref[...]
    o_ref[0, 0:tm - N_META, :] = _rms(a_ref[0, N_META:tm, :], g, D_MODEL)
    o_ref[0, tm - N_META:tm, :] = _rms(t_ref[0], g, D_MODEL)


def _final_norm(h3, g, seq):
    bsz = h3.shape[0]
    tm = _pick(seq, (512, 128))
    return pl.pallas_call(
        functools.partial(_final_kernel, tm=tm),
        grid=(bsz, seq // tm),
        in_specs=[
            pl.BlockSpec((1, tm, D_MODEL), lambda b, i: (b, i, 0)),
            pl.BlockSpec((1, N_META, D_MODEL), lambda b, i: (b, (i + 1) * (tm // N_META), 0)),
            pl.BlockSpec((1, D_MODEL), lambda b, i: (0, 0)),
        ],
        out_specs=pl.BlockSpec((1, tm, D_MODEL), lambda b, i: (b, i, 0)),
        out_shape=jax.ShapeDtypeStruct((bsz, seq, D_MODEL), F32),
        compiler_params=_params("parallel", "parallel"),
        name="final_norm",
    )(h3, h3, g)


def _rot_cols(w):
    half = w.shape[1] // 2
    return jnp.concatenate([-w[:, half:], w[:, :half]], axis=1)


def _prep_in_proj(w_in):
    zeros = lambda n: jnp.zeros((D_MODEL, n), F32)
    cuts = (512, 1536, 1544, 1800, 2056, 2312, 2504, 2632)
    z, xbc, dt, q_sb, k_sb, v_sb, q_a, c_kv, k_r = jnp.split(w_in, cuts, axis=1)
    kr1 = jnp.concatenate([zeros(MLA_NOPE), k_r, zeros(LANES - MLA_NOPE - MLA_ROPE)], axis=1)
    kr2 = jnp.concatenate([zeros(MLA_NOPE), _rot_cols(k_r), zeros(LANES - MLA_NOPE - MLA_ROPE)], axis=1)
    w = jnp.concatenate([xbc, z, q_sb * (64 ** -0.5), k_sb, v_sb, q_a, zeros(256 - MLA_Q_RANK), c_kv,
                         kr1, kr2, jnp.repeat(dt, SSD_HEAD_DIM, axis=1)], axis=1)
    return w.astype(BF16)


def _prep_mla(w_uq, w_ukv):
    qh = w_uq.reshape(MLA_Q_RANK, MLA_HEADS, MLA_NOPE + MLA_ROPE)
    pad = jnp.zeros((MLA_Q_RANK, MLA_HEADS, LANES - MLA_NOPE - MLA_ROPE), F32)
    nope0 = jnp.zeros((MLA_Q_RANK, MLA_HEADS, MLA_NOPE), F32)
    rope = qh[..., MLA_NOPE:]
    rope_rot = jnp.concatenate([-rope[..., MLA_ROPE // 2:], rope[..., :MLA_ROPE // 2]], axis=-1)
    wq = jnp.concatenate([qh, pad], axis=-1).reshape(MLA_Q_RANK, MLA_HEADS * LANES)
    wqr = jnp.concatenate([nope0, rope_rot, pad], axis=-1).reshape(MLA_Q_RANK, MLA_HEADS * LANES)
    rowpad = jnp.zeros((256 - MLA_Q_RANK, MLA_HEADS * LANES), F32)
    wq = jnp.concatenate([wq, rowpad], axis=0).astype(BF16)
    wqr = jnp.concatenate([wqr, rowpad], axis=0).astype(BF16)
    kvh = w_ukv.reshape(MLA_KV_RANK, MLA_HEADS, MLA_NOPE + 64)
    kpad = jnp.zeros((MLA_KV_RANK, MLA_HEADS, LANES - MLA_NOPE), F32)
    wk = jnp.concatenate([kvh[..., :MLA_NOPE], kpad], axis=-1).reshape(MLA_KV_RANK, MLA_HEADS * LANES)
    wvt = kvh[..., MLA_NOPE:].reshape(MLA_KV_RANK, MLA_HEADS * 64).T
    return wq, wqr, wk.astype(BF16), wvt.astype(BF16)


def _interleave_ff(a):
    lead = a.shape[:-1]
    g = a[..., :D_FF].reshape(lead + (D_FF // FF_CHUNK, FF_CHUNK))
    v = a[..., D_FF:].reshape(lead + (D_FF // FF_CHUNK, FF_CHUNK))
    return jnp.concatenate([g, v], axis=-1).reshape(lead + (2 * D_FF,))


def _rope_tables(lp):
    pos = jnp.arange(lp, dtype=F32)
    inv = 1.0 / (ROPE_BASE ** (jnp.arange(0, MLA_ROPE, 2, dtype=F32) / MLA_ROPE))
    ang = pos[:, None] * inv[None, :]
    ang = jnp.concatenate([ang, ang], axis=-1)
    ones = jnp.ones((lp, MLA_NOPE), F32)
    tail = LANES - MLA_NOPE - MLA_ROPE
    cosx = jnp.concatenate([ones, jnp.cos(ang), jnp.ones((lp, tail), F32)], axis=1)
    sinx = jnp.concatenate([0 * ones, jnp.sin(ang), jnp.zeros((lp, tail), F32)], axis=1)
    return cosx, sinx


def kernel(x, meta_tokens, norm_mix_g, w_in, ssd_conv_w, ssd_conv_b, ssd_dt_bias, ssd_a_log, ssd_d, ssd_norm_g, sb_norm_g, mla_q_norm_g, mla_kv_norm_g, mla_w_uq, mla_w_ukv, mla_norm_g, w_out, norm_ffn_g, ffn_w_up, ffn_conv_w, ffn_conv_b, ffn_w_down, final_norm_g):
    bsz, seq, _ = x.shape
    length = N_META + seq
    lp = -(-length // SEQ_ALIGN) * SEQ_ALIGN
    depth = w_in.shape[0]

    meta = jnp.broadcast_to(meta_tokens[None].astype(x.dtype), (bsz, N_META, D_MODEL))
    h = jnp.concatenate([meta, x, jnp.zeros((bsz, lp - length, D_MODEL), x.dtype)], axis=1)

    cosx, sinx = _rope_tables(lp)
    later_t = (lax.broadcasted_iota(jnp.int32, (ATT_SUB, ATT_SUB), 1) >=
               lax.broadcasted_iota(jnp.int32, (ATT_SUB, ATT_SUB), 0)).astype(BF16)
    row = lambda v: v.reshape(1, -1)
    rep = lambda v: jnp.repeat(v, SSD_HEAD_DIM).reshape(1, -1)

    for l in range(depth):
        u, dtx = _in_proj(h.reshape(bsz * lp, D_MODEL), row(norm_mix_g[l]), _prep_in_proj(w_in[l]))
        u3 = u.reshape(bsz, lp, U_END)
        y_ssd = _ssd(u3, dtx.reshape(bsz, lp, SSD_WIDTH), ssd_conv_w[l], row(ssd_conv_b[l]),
                     rep(ssd_dt_bias[l]), rep(ssd_a_log[l]), rep(ssd_d[l]), row(ssd_norm_g[l]))
        vt_sb = jnp.swapaxes(u3[:, :, U_VSB:U_VSB + SB_WIDTH], 1, 2)
        o_sb = _sb_attention(u3, vt_sb, later_t)
        wq, wqr, wk, wvt = _prep_mla(mla_w_uq[l], mla_w_ukv[l])
        gq = jnp.concatenate([mla_q_norm_g[l], jnp.zeros((256 - MLA_Q_RANK,), F32)]).reshape(1, -1)
        qc, kc, vt_mla = _mla_proj(u, cosx, sinx, gq, row(mla_kv_norm_g[l]), wq, wqr, wk, wvt, bsz, lp)
        o_mla = _mla_attention(qc.reshape(bsz, lp, -1), kc.reshape(bsz, lp, -1), vt_mla)
        h = _out_ffn(h, y_ssd, o_sb, o_mla, row(sb_norm_g[l]), row(mla_norm_g[l]),
                     w_out[l].astype(BF16), row(norm_ffn_g[l]),
                     _interleave_ff(ffn_w_up[l]).astype(BF16), _interleave_ff(ffn_conv_w[l]),
                     row(_interleave_ff(ffn_conv_b[l])), ffn_w_down[l].astype(BF16))

    return _final_norm(h, row(final_norm_g), seq)
```

```python
import functools
import math

import jax
import jax.numpy as jnp
from jax import lax
from jax.experimental import pallas as pl
from jax.experimental.pallas import tpu as pltpu

F32 = jnp.float32
BF16 = jnp.bfloat16

D_MODEL = 1024
N_META = 16
EPS = 1e-6
SSD_HEADS = 8
SSD_HEAD_DIM = 64
SSD_WIDTH = 512
SSD_STATE = 128
SSD_XBC = 1024
SSD_CHUNK = 128
SB_HEADS = 4
SB_HEAD_DIM = 64
SB_WIDTH = 256
MLA_HEADS = 4
MLA_V = 64
MLA_NOPE = 64
MLA_ROPE = 32
MLA_Q_RANK = 192
MLA_KV_RANK = 128
ROPE_BASE = 10000.0
D_FF = 2816
FF_CHUNK = 256

LANES = 128
ATT_SUB = 256
ATT_TQ = 256
PIPE_UNROLL = 4
SEQ_ALIGN = ATT_TQ
VMEM_LIMIT = 56 * 1024 * 1024

U_XBC, U_Z, U_QSB, U_KSB, U_VSB, U_QA, U_CKV, U_KR1, U_KR2, U_END = (
    0, 1024, 1536, 1792, 2048, 2304, 2560, 2688, 2816, 2944)

NT_DIMS = (((1,), (1,)), ((), ()))
LOG2E = 1.4426950408889634


def _pick(n, candidates):
    for c in candidates:
        if n % c == 0:
            return c
    raise ValueError(f"no tile in {candidates} divides {n}")


def _softplus(x):
    return jnp.maximum(x, 0.0) + jnp.log(1.0 + jnp.exp2(jnp.abs(x) * (-LOG2E)))


def _silu(x):
    return x * jax.nn.sigmoid(x)


def _rms(x, g, n):
    ms = jnp.sum(x * x, axis=-1, keepdims=True) * (1.0 / n)
    return (x * lax.rsqrt(ms + EPS)) * g


def _params(*sem):
    return pltpu.CompilerParams(dimension_semantics=sem, vmem_limit_bytes=VMEM_LIMIT)


def _in_proj_kernel(x_ref, g_ref, w_ref, u_ref, dt_ref):
    xn = _rms(x_ref[...], g_ref[...], D_MODEL).astype(BF16)
    for n0 in range(0, U_END, 512):
        n1 = min(n0 + 512, U_END)
        u_ref[:, n0:n1] = jnp.dot(xn, w_ref[:, n0:n1], preferred_element_type=F32).astype(BF16)
    dt_ref[...] = jnp.dot(xn, w_ref[:, U_END:U_END + SSD_WIDTH], preferred_element_type=F32)


def _in_proj(h2d, g, w):
    rows = h2d.shape[0]
    tm = _pick(rows, (512, 256))
    return pl.pallas_call(
        _in_proj_kernel,
        grid=(rows // tm,),
        in_specs=[
            pl.BlockSpec((tm, D_MODEL), lambda i: (i, 0)),
            pl.BlockSpec((1, D_MODEL), lambda i: (0, 0)),
            pl.BlockSpec(w.shape, lambda i: (0, 0), pipeline_mode=pl.Buffered(1)),
        ],
        out_specs=[
            pl.BlockSpec((tm, U_END), lambda i: (i, 0)),
            pl.BlockSpec((tm, SSD_WIDTH), lambda i: (i, 0)),
        ],
        out_shape=[
            jax.ShapeDtypeStruct((rows, U_END), BF16),
            jax.ShapeDtypeStruct((rows, SSD_WIDTH), F32),
        ],
        compiler_params=_params("parallel"),
        name="in_proj",
    )(h2d, g, w)


def _cumsum_rows(x):
    n = x.shape[0]
    row = lax.broadcasted_iota(jnp.int32, x.shape, 0)
    s = 1
    while s < n:
        x = x + jnp.where(row >= s, pltpu.roll(x, s, 0), 0.0)
        s *= 2
    return x


def _ssd_kernel(z_ref, xbc_ref, dt_ref, cw_ref, cb_ref, dtb_ref, alog_ref, d_ref, g_ref,
                y_ref, ext_ref, st_ref):
    T = SSD_CHUNK

    @pl.when(pl.program_id(1) == 0)
    def _():
        ext_ref[0:8, :] = jnp.zeros((8, SSD_XBC), F32)
        st_ref[...] = jnp.zeros(st_ref.shape, F32)

    ext_ref[8:8 + T, :] = xbc_ref[0].astype(F32)
    cw = cw_ref[...]
    conv = cb_ref[...] + ext_ref[8:8 + T, :] * cw[3:4]
    for k in range(3):
        conv = conv + ext_ref[5 + k:5 + k + T, :] * cw[k:k + 1]
    ext_ref[0:8, :] = ext_ref[T:T + 8, :]
    xc = _silu(conv)
    xs = xc[:, :SSD_WIDTH]
    bm = xc[:, SSD_WIDTH:SSD_WIDTH + 2 * SSD_STATE]
    cm = xc[:, SSD_WIDTH + 2 * SSD_STATE:]

    dt = _softplus(dt_ref[0] + dtb_ref[...])
    acs = _cumsum_rows(dt * (-jnp.exp(alog_ref[...])))
    acs_last = acs[T - 1:T, :]
    x_dt = xs * dt
    xb = x_dt.astype(BF16)
    xd = (x_dt * jnp.exp(acs_last - acs)).astype(BF16)
    eacs = jnp.exp(acs)
    acs_t = acs.T

    tril = (lax.broadcasted_iota(jnp.int32, (T, T), 0) >= lax.broadcasted_iota(jnp.int32, (T, T), 1))
    low_half = lax.broadcasted_iota(jnp.int32, (T, LANES), 1) < SSD_HEAD_DIM
    y_groups = []
    for g in range(2):
        bg = bm[:, g * SSD_STATE:(g + 1) * SSD_STATE]
        cg = cm[:, g * SSD_STATE:(g + 1) * SSD_STATE].astype(BF16)
        cb = lax.dot_general(cg, bg.astype(BF16), NT_DIMS, preferred_element_type=F32)
        pair_out = []
        for pr in range(2):
            lane0 = g * 256 + pr * LANES
            xp = xb[:, lane0:lane0 + LANES]
            res = []
            for hh in range(2):
                hl = lane0 + hh * SSD_HEAD_DIM
                seg = acs[:, hl:hl + 1] - acs_t[hl:hl + 1, :]
                decay = jnp.where(tril, jnp.exp(seg), 0.0)
                res.append(jnp.dot((cb * decay).astype(BF16), xp, preferred_element_type=F32))
            pair_out.append(jnp.where(low_half, res[0], res[1]))
        y_diag = jnp.concatenate(pair_out, axis=1)
        hg = st_ref[g]
        y_off = jnp.dot(cg, hg.astype(BF16), preferred_element_type=F32) * eacs[:, g * 256:(g + 1) * 256]
        st_ref[g] = hg * jnp.exp(acs_last[:, g * 256:(g + 1) * 256]) + jnp.dot(
            bg.T.astype(BF16), xd[:, g * 256:(g + 1) * 256], preferred_element_type=F32)
        y_groups.append(y_diag + y_off)
    y = jnp.concatenate(y_groups, axis=1) + xs * d_ref[...]
    y = y * _silu(z_ref[0].astype(F32))
    y_ref[0] = _rms(y, g_ref[...], SSD_WIDTH).astype(BF16)


def _ssd(u3, dtx3, cw, cb, dtb, alog, dskip, g):
    bsz, lp, _ = u3.shape
    T = SSD_CHUNK
    vec = lambda n: pl.BlockSpec((1, n), lambda b, c: (0, 0))
    return pl.pallas_call(
        _ssd_kernel,
        grid=(bsz, lp // T),
        in_specs=[
            pl.BlockSpec((1, T, SSD_WIDTH), lambda b, c: (b, c, U_Z // SSD_WIDTH)),
            pl.BlockSpec((1, T, SSD_XBC), lambda b, c: (b, c, U_XBC // SSD_XBC)),
            pl.BlockSpec((1, T, SSD_WIDTH), lambda b, c: (b, c, 0)),
            pl.BlockSpec((4, SSD_XBC), lambda b, c: (0, 0)),
            vec(SSD_XBC), vec(SSD_WIDTH), vec(SSD_WIDTH), vec(SSD_WIDTH), vec(SSD_WIDTH),
        ],
        out_specs=pl.BlockSpec((1, T, SSD_WIDTH), lambda b, c: (b, c, 0)),
        out_shape=jax.ShapeDtypeStruct((bsz, lp, SSD_WIDTH), BF16),
        scratch_shapes=[pltpu.VMEM((T + 8, SSD_XBC), F32), pltpu.VMEM((2, SSD_STATE, 256), F32)],
        compiler_params=_params("parallel", "arbitrary"),
        name="ssd_scan",
    )(u3, u3, dtx3, cw, cb, dtb, alog, dskip, g)


def _key_start(qi, n, tq, lp):
    blk = (tq // ATT_SUB) * (qi + 1) - 1 - n
    return pl.multiple_of(jnp.clip(blk, 0, lp // ATT_SUB - 1) * ATT_SUB, ATT_SUB)


def _causal_masks(tq, strict):
    key_i = lax.broadcasted_iota(jnp.int32, (ATT_SUB, tq), 0)
    qry_i = lax.broadcasted_iota(jnp.int32, (ATT_SUB, tq), 1)
    nd = tq // ATT_SUB
    offs = [(nd - 1 - n) * ATT_SUB for n in range(nd)]
    return [(key_i + o < qry_i) if strict else (key_i + o <= qry_i) for o in offs]


def _head_queries(q, heads):
    lane = lax.broadcasted_iota(jnp.int32, (q.shape[0], LANES), 1)
    out = []
    for h in range(heads):
        pair = q[:, (h // 2) * LANES:(h // 2 + 1) * LANES].astype(F32)
        lo = 64 * (h % 2)
        out.append(jnp.where((lane >= lo) & (lane < lo + 64), pair, 0.0).astype(BF16))
    return out


def _pipeline3(nblk, ndiag, heads, stage1, stage2, stage3):
    diag1 = 1 if ndiag > 1 else None
    stage1(0, 0, 0)
    stage2(0, 0)
    stage1(1, 1, diag1)
    stage3(0, 0, True)
    if ndiag > 1:
        stage2(1, 1)
    else:
        stage2(1, None, nblk >= 2)
    stage1(2, 0)

    def step(n, par):
        for h in range(heads):
            stage1(n + 2, par, None, (h,))
            stage3(n, par, False, (h,))
            stage2(1 - par, None, None, (h,))

    def group(n0, count):
        for j in range(count):
            step(n0 + j, 1 - j % 2)

    steady = jnp.maximum(nblk - 3, 0)
    whole = steady // PIPE_UNROLL

    def body(i, c):
        group(1 + PIPE_UNROLL * i, PIPE_UNROLL)
        return c

    lax.fori_loop(0, whole, body, 0)
    rest = steady - whole * PIPE_UNROLL
    size = PIPE_UNROLL // 2
    while size >= 1:
        done = rest - rest % (2 * size)

        @pl.when(rest % (2 * size) >= size)
        def _(size=size, done=done):
            group(1 + whole * PIPE_UNROLL + done, size)

        size //= 2

    @pl.when(nblk >= 3)
    def _():
        par = (nblk - 2) % 2
        stage3(nblk - 2, par, False)
        stage2(1 - par, None)
        stage3(nblk - 1, 1 - par, False)

    @pl.when(nblk == 2)
    def _():
        stage3(1, 1, False)


def _sb_kernel(q_ref, k_ref, vt_ref, lt_ref, o_ref, s_scr, w_scr, tot_scr, acc_scr, r_scr, *, tq, lp):
    qi = pl.program_id(1)
    ndiag = tq // ATT_SUB
    nblk = ndiag * (qi + 1)
    qms = _head_queries(q_ref[0], SB_HEADS)
    later_t = lt_ref[...]
    masks = _causal_masks(tq, strict=True)
    pair = lambda h: slice((h // 2) * LANES, (h // 2 + 1) * LANES)
    rows = lambda h: slice(h * SB_HEAD_DIM, (h + 1) * SB_HEAD_DIM)

    all_heads = tuple(range(SB_HEADS))

    def stage1(n, par, diag=None, heads=all_heads):
        ks = _key_start(qi, n, tq, lp)
        for h in heads:
            s_scr[par, h] = lax.dot_general(k_ref[0, pl.ds(ks, ATT_SUB), pair(h)], qms[h], NT_DIMS,
                                            preferred_element_type=F32)

    def stage2(par, diag, exists=None, heads=all_heads):
        for h in heads:
            s = s_scr[par, h]
            sp = _softplus(s)
            if diag is not None:
                sp = jnp.where(masks[diag], sp, 0.0)
            loc = jnp.dot(later_t, sp.astype(BF16), preferred_element_type=F32)
            w = jnp.exp(s - loc)
            if diag is not None:
                w = jnp.where(masks[diag], w, 0.0)
            w_scr[par, h] = w.astype(BF16)
            tot_scr[par, h] = loc[0:1, :]

    def stage3(n, par, first, heads=all_heads):
        ks = _key_start(qi, n, tq, lp)
        for h in heads:
            pv = jnp.dot(vt_ref[0, rows(h), pl.ds(ks, ATT_SUB)], w_scr[par, h], preferred_element_type=F32)
            if first:
                acc_scr[rows(h), :] = pv
                r_scr[h] = -tot_scr[par, h]
            else:
                r = r_scr[h]
                acc_scr[rows(h), :] = acc_scr[rows(h), :] + jnp.exp(r) * pv
                r_scr[h] = r - tot_scr[par, h]

    _pipeline3(nblk, ndiag, SB_HEADS, stage1, stage2, stage3)
    o_ref[0] = acc_scr[...].T.astype(BF16)


def _sb_attention(u3, vt, later_t):
    bsz, lp, _ = u3.shape
    tq = ATT_TQ
    return pl.pallas_call(
        functools.partial(_sb_kernel, tq=tq, lp=lp),
        grid=(bsz, pl.cdiv(lp, tq)),
        in_specs=[
            pl.BlockSpec((1, tq, SB_WIDTH), lambda b, i: (b, i, U_QSB // SB_WIDTH)),
            pl.BlockSpec((1, lp, SB_WIDTH), lambda b, i: (b, 0, U_KSB // SB_WIDTH)),
            pl.BlockSpec((1, SB_WIDTH, lp), lambda b, i: (b, 0, 0)),
            pl.BlockSpec((ATT_SUB, ATT_SUB), lambda b, i: (0, 0)),
        ],
        out_specs=pl.BlockSpec((1, tq, SB_WIDTH), lambda b, i: (b, i, 0)),
        out_shape=jax.ShapeDtypeStruct((bsz, lp, SB_WIDTH), BF16),
        scratch_shapes=[pltpu.VMEM((2, SB_HEADS, ATT_SUB, tq), F32), pltpu.VMEM((2, SB_HEADS, ATT_SUB, tq), BF16),
                        pltpu.VMEM((2, SB_HEADS, 1, tq), F32), pltpu.VMEM((SB_WIDTH, tq), F32),
                        pltpu.VMEM((SB_HEADS, 1, tq), F32)],
        compiler_params=_params("parallel", "arbitrary"),
        name="sb_attention",
    )(u3, u3, vt, later_t)


def _mla_proj_kernel(qa_ref, ckv_ref, kr1_ref, kr2_ref, cos_ref, sin_ref, gq_ref, gkv_ref,
                     wq_ref, wqr_ref, wk_ref, wvt_ref, qc_ref, kc_ref, vt_ref):
    cos = cos_ref[...]
    sin = sin_ref[...]
    cos4 = jnp.concatenate([cos] * MLA_HEADS, axis=1)
    sin4 = jnp.concatenate([sin] * MLA_HEADS, axis=1)
    qn = _rms(qa_ref[...].astype(F32), gq_ref[...], MLA_Q_RANK).astype(BF16)
    q1 = jnp.dot(qn, wq_ref[...], preferred_element_type=F32)
    q2 = jnp.dot(qn, wqr_ref[...], preferred_element_type=F32)
    scale = (MLA_NOPE + MLA_ROPE) ** -0.5 * LOG2E
    qc_ref[...] = ((q1 * cos4 + q2 * sin4) * scale).astype(BF16)
    cn = _rms(ckv_ref[...].astype(F32), gkv_ref[...], MLA_KV_RANK).astype(BF16)
    kn = jnp.dot(cn, wk_ref[...], preferred_element_type=F32)
    krope = kr1_ref[...].astype(F32) * cos + kr2_ref[...].astype(F32) * sin
    kc_ref[...] = (kn + jnp.concatenate([krope] * MLA_HEADS, axis=1)).astype(BF16)
    vt_ref[0] = lax.dot_general(wvt_ref[...], cn, NT_DIMS, preferred_element_type=F32).astype(BF16)


def _mla_proj(u2d, cosx, sinx, gq, gkv, wq, wqr, wk, wvt, bsz, lp):
    rows = u2d.shape[0]
    tm = _pick(lp, (768, 512, 256))
    nper = lp // tm
    const = lambda a: pl.BlockSpec(a.shape, lambda i: (0, 0))
    return pl.pallas_call(
        _mla_proj_kernel,
        grid=(rows // tm,),
        in_specs=[
            pl.BlockSpec((tm, 256), lambda i: (i, U_QA // 256)),
            pl.BlockSpec((tm, LANES), lambda i: (i, U_CKV // LANES)),
            pl.BlockSpec((tm, LANES), lambda i: (i, U_KR1 // LANES)),
            pl.BlockSpec((tm, LANES), lambda i: (i, U_KR2 // LANES)),
            pl.BlockSpec((tm, LANES), lambda i: (i % nper, 0)),
            pl.BlockSpec((tm, LANES), lambda i: (i % nper, 0)),
            const(gq), const(gkv), const(wq), const(wqr), const(wk), const(wvt),
        ],
        out_specs=[
            pl.BlockSpec((tm, MLA_HEADS * LANES), lambda i: (i, 0)),
            pl.BlockSpec((tm, MLA_HEADS * LANES), lambda i: (i, 0)),
            pl.BlockSpec((1, 256, tm), lambda i: (i // nper, 0, i % nper)),
        ],
        out_shape=[
            jax.ShapeDtypeStruct((rows, MLA_HEADS * LANES), BF16),
            jax.ShapeDtypeStruct((rows, MLA_HEADS * LANES), BF16),
            jax.ShapeDtypeStruct((bsz, 256, lp), BF16),
        ],
        compiler_params=_params("parallel"),
        name="mla_proj",
    )(u2d, u2d, u2d, u2d, cosx, sinx, gq, gkv, wq, wqr, wk, wvt)


def _mla_kernel(q_ref, k_ref, vt_ref, o_ref, s_scr, cmax_scr, p_scr, alpha_scr, acc_scr, m_scr, l_scr,
                *, tq, lp):
    qi = pl.program_id(1)
    ndiag = tq // ATT_SUB
    nblk = ndiag * (qi + 1)
    qs = [q_ref[0, :, h * LANES:(h + 1) * LANES] for h in range(MLA_HEADS)]
    masks = _causal_masks(tq, strict=False)
    rows = lambda h: slice(h * MLA_V, (h + 1) * MLA_V)

    all_heads = tuple(range(MLA_HEADS))

    def stage1(n, par, diag=None, heads=all_heads):
        ks = _key_start(qi, n, tq, lp)
        for h in heads:
            s = lax.dot_general(k_ref[0, pl.ds(ks, ATT_SUB), h * LANES:(h + 1) * LANES], qs[h],
                                NT_DIMS, preferred_element_type=F32)
            if diag is not None:
                s = jnp.where(masks[diag], s, -1e30)
            s_scr[par, h] = s
            cmax_scr[par, h] = jnp.max(s, axis=0, keepdims=True)

    def stage2(par, diag, exists=None, heads=all_heads):
        first = diag == 0
        for h in heads:
            if first:
                m_new = cmax_scr[par, h]
            else:
                m_old = m_scr[h]
                m_new = jnp.maximum(m_old, cmax_scr[par, h])
            p = jnp.exp2(s_scr[par, h] - m_new)
            p_scr[par, h] = p.astype(BF16)
            l_new = jnp.sum(p, axis=0, keepdims=True)
            if not first:
                alpha = jnp.exp2(m_old - m_new)
                alpha_scr[par, h] = alpha
                l_new = alpha * l_scr[h] + l_new
                if exists is not None:
                    l_new = jnp.where(exists, l_new, l_scr[h])
                    m_new = jnp.where(exists, m_new, m_old)
            l_scr[h] = l_new
            m_scr[h] = m_new

    def stage3(n, par, first, heads=all_heads):
        ks = _key_start(qi, n, tq, lp)
        for h in heads:
            pv = jnp.dot(vt_ref[0, rows(h), pl.ds(ks, ATT_SUB)], p_scr[par, h], preferred_element_type=F32)
            if first:
                acc_scr[rows(h), :] = pv
            else:
                acc_scr[rows(h), :] = alpha_scr[par, h] * acc_scr[rows(h), :] + pv

    _pipeline3(nblk, ndiag, MLA_HEADS, stage1, stage2, stage3)
    for h in range(MLA_HEADS):
        acc_scr[rows(h), :] = acc_scr[rows(h), :] / l_scr[h]
    o_ref[0] = acc_scr[...].T.astype(BF16)


def _mla_attention(qc3, kc3, vt):
    bsz, lp, width = qc3.shape
    tq = ATT_TQ
    out_w = MLA_HEADS * MLA_V
    return pl.pallas_call(
        functools.partial(_mla_kernel, tq=tq, lp=lp),
        grid=(bsz, pl.cdiv(lp, tq)),
        in_specs=[
            pl.BlockSpec((1, tq, width), lambda b, i: (b, i, 0)),
            pl.BlockSpec((1, lp, width), lambda b, i: (b, 0, 0)),
            pl.BlockSpec((1, out_w, lp), lambda b, i: (b, 0, 0)),
        ],
        out_specs=pl.BlockSpec((1, tq, out_w), lambda b, i: (b, i, 0)),
        out_shape=jax.ShapeDtypeStruct((bsz, lp, out_w), BF16),
        scratch_shapes=[pltpu.VMEM((2, MLA_HEADS, ATT_SUB, tq), F32), pltpu.VMEM((2, MLA_HEADS, 1, tq), F32),
                        pltpu.VMEM((2, MLA_HEADS, ATT_SUB, tq), BF16),
                        pltpu.VMEM((2, MLA_HEADS, 1, tq), F32), pltpu.VMEM((out_w, tq), F32),
                        pltpu.VMEM((MLA_HEADS, 1, tq), F32), pltpu.VMEM((MLA_HEADS, 1, tq), F32)],
        compiler_params=_params("parallel", "arbitrary"),
        name="mla_attention",
    )(qc3, kc3, vt)


def _out_ffn_kernel(h_ref, yssd_ref, osb_ref, omla_ref, gsb_ref, gmla_ref, wout_ref, gffn_ref,
                    wup_ref, cw_ref, cb_ref, wdown_ref, o_ref, halo_ref, ext_ref, act_ref, *, tm):
    @pl.when(pl.program_id(1) == 0)
    def _():
        halo_ref[...] = jnp.zeros(halo_ref.shape, F32)

    ysb = _rms(osb_ref[0].astype(F32), gsb_ref[...], SB_WIDTH).astype(BF16)
    ymla = _rms(omla_ref[0].astype(F32), gmla_ref[...], 256).astype(BF16)
    mix = jnp.dot(yssd_ref[0], wout_ref[0:512, :], preferred_element_type=F32)
    mix = mix + jnp.dot(ysb, wout_ref[512:768, :], preferred_element_type=F32)
    mix = mix + jnp.dot(ymla, wout_ref[768:1024, :], preferred_element_type=F32)
    h1 = h_ref[0] + mix
    o_ref[0] = h1
    xn = _rms(h1, gffn_ref[...], D_MODEL).astype(BF16)

    w2 = 2 * FF_CHUNK
    for c in range(D_FF // FF_CHUNK):
        cols = slice(c * w2, (c + 1) * w2)
        ext_ref[0:8, :] = halo_ref[:, cols]
        ext_ref[8:8 + tm, :] = jnp.dot(xn, wup_ref[:, cols], preferred_element_type=F32)
        halo_ref[:, cols] = ext_ref[tm:tm + 8, :]
        cw = cw_ref[:, cols]
        conv = cb_ref[:, cols] + ext_ref[8:8 + tm, :] * cw[2:3]
        conv = conv + ext_ref[7:7 + tm, :] * cw[1:2]
        conv = conv + ext_ref[6:6 + tm, :] * cw[0:1]
        act = _silu(conv[:, :FF_CHUNK]) * conv[:, FF_CHUNK:]
        act_ref[:, c * FF_CHUNK:(c + 1) * FF_CHUNK] = act.astype(BF16)
    o_ref[0] = o_ref[0] + jnp.dot(act_ref[...], wdown_ref[...], preferred_element_type=F32)


def _out_ffn(h3, yssd, osb, omla, gsb, gmla, wout, gffn, wup, cw, cb, wdown):
    bsz, lp, _ = h3.shape
    tm = _pick(lp, (768, 512, 256))
    const = lambda a: pl.BlockSpec(a.shape, lambda b, t: (0, 0), pipeline_mode=pl.Buffered(1))
    rowblk = lambda n: pl.BlockSpec((1, tm, n), lambda b, t: (b, t, 0))
    return pl.pallas_call(
        functools.partial(_out_ffn_kernel, tm=tm),
        grid=(bsz, lp // tm),
        in_specs=[rowblk(D_MODEL), rowblk(SSD_WIDTH), rowblk(256), rowblk(256),
                  const(gsb), const(gmla), const(wout), const(gffn),
                  const(wup), const(cw), const(cb), const(wdown)],
        out_specs=rowblk(D_MODEL),
        out_shape=jax.ShapeDtypeStruct(h3.shape, F32),
        scratch_shapes=[pltpu.VMEM((8, 2 * D_FF), F32),
                        pltpu.VMEM((tm + 8, 2 * FF_CHUNK), F32),
                        pltpu.VMEM((tm, D_FF), BF16)],
        compiler_params=_params("parallel", "arbitrary"),
        name="out_ffn",
    )(h3, yssd, osb, omla, gsb, gmla, wout, gffn, wup, cw, cb, wdown)


def _final_kernel(a_ref, t_ref, g_ref, o_ref, *, tm):
    g = g_ref[...]
    o_ref[0, 0:tm - N_META, :] = _rms(a_ref[0, N_META:tm, :], g, D_MODEL)
    o_ref[0, tm - N_META:tm, :] = _rms(t_ref[0], g, D_MODEL)


def _final_norm(h3, g, seq):
    bsz = h3.shape[0]
    tm = _pick(seq, (512, 128))
    return pl.pallas_call(
        functools.partial(_final_kernel, tm=tm),
        grid=(bsz, seq // tm),
        in_specs=[
            pl.BlockSpec((1, tm, D_MODEL), lambda b, i: (b, i, 0)),
            pl.BlockSpec((1, N_META, D_MODEL), lambda b, i: (b, (i + 1) * (tm // N_META), 0)),
            pl.BlockSpec((1, D_MODEL), lambda b, i: (0, 0)),
        ],
        out_specs=pl.BlockSpec((1, tm, D_MODEL), lambda b, i: (b, i, 0)),
        out_shape=jax.ShapeDtypeStruct((bsz, seq, D_MODEL), F32),
        compiler_params=_params("parallel", "parallel"),
        name="final_norm",
    )(h3, h3, g)


def _rot_cols(w):
    half = w.shape[1] // 2
    return jnp.concatenate([-w[:, half:], w[:, :half]], axis=1)


def _prep_in_proj(w_in):
    zeros = lambda n: jnp.zeros((D_MODEL, n), F32)
    cuts = (512, 1536, 1544, 1800, 2056, 2312, 2504, 2632)
    z, xbc, dt, q_sb, k_sb, v_sb, q_a, c_kv, k_r = jnp.split(w_in, cuts, axis=1)
    kr1 = jnp.concatenate([zeros(MLA_NOPE), k_r, zeros(LANES - MLA_NOPE - MLA_ROPE)], axis=1)
    kr2 = jnp.concatenate([zeros(MLA_NOPE), _rot_cols(k_r), zeros(LANES - MLA_NOPE - MLA_ROPE)], axis=1)
    w = jnp.concatenate([xbc, z, q_sb * (64 ** -0.5), k_sb, v_sb, q_a, zeros(256 - MLA_Q_RANK), c_kv,
                         kr1, kr2, jnp.repeat(dt, SSD_HEAD_DIM, axis=1)], axis=1)
    return w.astype(BF16)


def _prep_mla(w_uq, w_ukv):
    qh = w_uq.reshape(MLA_Q_RANK, MLA_HEADS, MLA_NOPE + MLA_ROPE)
    pad = jnp.zeros((MLA_Q_RANK, MLA_HEADS, LANES - MLA_NOPE - MLA_ROPE), F32)
    nope0 = jnp.zeros((MLA_Q_RANK, MLA_HEADS, MLA_NOPE), F32)
    rope = qh[..., MLA_NOPE:]
    rope_rot = jnp.concatenate([-rope[..., MLA_ROPE // 2:], rope[..., :MLA_ROPE // 2]], axis=-1)
    wq = jnp.concatenate([qh, pad], axis=-1).reshape(MLA_Q_RANK, MLA_HEADS * LANES)
    wqr = jnp.concatenate([nope0, rope_rot, pad], axis=-1).reshape(MLA_Q_RANK, MLA_HEADS * LANES)
    rowpad = jnp.zeros((256 - MLA_Q_RANK, MLA_HEADS * LANES), F32)
    wq = jnp.concatenate([wq, rowpad], axis=0).astype(BF16)
    wqr = jnp.concatenate([wqr, rowpad], axis=0).astype(BF16)
    kvh = w_ukv.reshape(MLA_KV_RANK, MLA_HEADS, MLA_NOPE + 64)
    kpad = jnp.zeros((MLA_KV_RANK, MLA_HEADS, LANES - MLA_NOPE), F32)
    wk = jnp.concatenate([kvh[..., :MLA_NOPE], kpad], axis=-1).reshape(MLA_KV_RANK, MLA_HEADS * LANES)
    wvt = kvh[..., MLA_NOPE:].reshape(MLA_KV_RANK, MLA_HEADS * 64).T
    return wq, wqr, wk.astype(BF16), wvt.astype(BF16)


def _interleave_ff(a):
    lead = a.shape[:-1]
    g = a[..., :D_FF].reshape(lead + (D_FF // FF_CHUNK, FF_CHUNK))
    v = a[..., D_FF:].reshape(lead + (D_FF // FF_CHUNK, FF_CHUNK))
    return jnp.concatenate([g, v], axis=-1).reshape(lead + (2 * D_FF,))


def _rope_tables(lp):
    pos = jnp.arange(lp, dtype=F32)
    inv = 1.0 / (ROPE_BASE ** (jnp.arange(0, MLA_ROPE, 2, dtype=F32) / MLA_ROPE))
    ang = pos[:, None] * inv[None, :]
    ang = jnp.concatenate([ang, ang], axis=-1)
    ones = jnp.ones((lp, MLA_NOPE), F32)
    tail = LANES - MLA_NOPE - MLA_ROPE
    cosx = jnp.concatenate([ones, jnp.cos(ang), jnp.ones((lp, tail), F32)], axis=1)
    sinx = jnp.concatenate([0 * ones, jnp.sin(ang), jnp.zeros((lp, tail), F32)], axis=1)
    return cosx, sinx


def kernel(x, meta_tokens, norm_mix_g, w_in, ssd_conv_w, ssd_conv_b, ssd_dt_bias, ssd_a_log, ssd_d, ssd_norm_g, sb_norm_g, mla_q_norm_g, mla_kv_norm_g, mla_w_uq, mla_w_ukv, mla_norm_g, w_out, norm_ffn_g, ffn_w_up, ffn_conv_w, ffn_conv_b, ffn_w_down, final_norm_g):
    bsz, seq, _ = x.shape
    length = N_META + seq
    lp = -(-length // SEQ_ALIGN) * SEQ_ALIGN
    depth = w_in.shape[0]

    meta = jnp.broadcast_to(meta_tokens[None].astype(x.dtype), (bsz, N_META, D_MODEL))
    h = jnp.concatenate([meta, x, jnp.zeros((bsz, lp - length, D_MODEL), x.dtype)], axis=1)

    cosx, sinx = _rope_tables(lp)
    later_t = (lax.broadcasted_iota(jnp.int32, (ATT_SUB, ATT_SUB), 1) >=
               lax.broadcasted_iota(jnp.int32, (ATT_SUB, ATT_SUB), 0)).astype(BF16)
    row = lambda v: v.reshape(1, -1)
    rep = lambda v: jnp.repeat(v, SSD_HEAD_DIM).reshape(1, -1)

    for l in range(depth):
        u, dtx = _in_proj(h.reshape(bsz * lp, D_MODEL), row(norm_mix_g[l]), _prep_in_proj(w_in[l]))
        u3 = u.reshape(bsz, lp, U_END)
        y_ssd = _ssd(u3, dtx.reshape(bsz, lp, SSD_WIDTH), ssd_conv_w[l], row(ssd_conv_b[l]),
                     rep(ssd_dt_bias[l]), rep(ssd_a_log[l]), rep(ssd_d[l]), row(ssd_norm_g[l]))
        vt_sb = jnp.swapaxes(u3[:, :, U_VSB:U_VSB + SB_WIDTH], 1, 2)
        o_sb = _sb_attention(u3, vt_sb, later_t)
        wq, wqr, wk, wvt = _prep_mla(mla_w_uq[l], mla_w_ukv[l])
        gq = jnp.concatenate([mla_q_norm_g[l], jnp.zeros((256 - MLA_Q_RANK,), F32)]).reshape(1, -1)
        qc, kc, vt_mla = _mla_proj(u, cosx, sinx, gq, row(mla_kv_norm_g[l]), wq, wqr, wk, wvt, bsz, lp)
        o_mla = _mla_attention(qc.reshape(bsz, lp, -1), kc.reshape(bsz, lp, -1), vt_mla)
        h = _out_ffn(h, y_ssd, o_sb, o_mla, row(sb_norm_g[l]), row(mla_norm_g[l]),
                     w_out[l].astype(BF16), row(norm_ffn_g[l]),
                     _interleave_ff(ffn_w_up[l]).astype(BF16), _interleave_ff(ffn_conv_w[l]),
                     row(_interleave_ff(ffn_conv_b[l])), ffn_w_down[l].astype(BF16))

    return _final_norm(h, row(final_norm_g), seq)
```

```python
import functools
import math

import jax
import jax.numpy as jnp
from jax import lax
from jax.experimental import pallas as pl
from jax.experimental.pallas import tpu as pltpu

F32 = jnp.float32
BF16 = jnp.bfloat16

D_MODEL = 1024
N_META = 16
EPS = 1e-6
SSD_HEADS = 8
SSD_HEAD_DIM = 64
SSD_WIDTH = 512
SSD_STATE = 128
SSD_XBC = 1024
SSD_CHUNK = 128
SB_HEADS = 4
SB_HEAD_DIM = 64
SB_WIDTH = 256
MLA_HEADS = 4
MLA_V = 64
MLA_NOPE = 64
MLA_ROPE = 32
MLA_Q_RANK = 192
MLA_KV_RANK = 128
ROPE_BASE = 10000.0
D_FF = 2816
FF_CHUNK = 256

LANES = 128
ATT_SUB = 256
ATT_TQ = 256
PIPE_UNROLL = 4
SEQ_ALIGN = ATT_TQ
VMEM_LIMIT = 56 * 1024 * 1024

U_XBC, U_Z, U_QSB, U_KSB, U_VSB, U_QA, U_CKV, U_KR1, U_KR2, U_END = (
    0, 1024, 1536, 1792, 2048, 2304, 2560, 2688, 2816, 2944)

NT_DIMS = (((1,), (1,)), ((), ()))
LOG2E = 1.4426950408889634


def _pick(n, candidates):
    for c in candidates:
        if n % c == 0:
            return c
    raise ValueError(f"no tile in {candidates} divides {n}")


def _softplus(x):
    return jnp.maximum(x, 0.0) + jnp.log(1.0 + jnp.exp2(jnp.abs(x) * (-LOG2E)))


def _silu(x):
    return x * jax.nn.sigmoid(x)


def _rms(x, g, n):
    ms = jnp.sum(x * x, axis=-1, keepdims=True) * (1.0 / n)
    return (x * lax.rsqrt(ms + EPS)) * g


def _params(*sem):
    return pltpu.CompilerParams(dimension_semantics=sem, vmem_limit_bytes=VMEM_LIMIT)


def _in_proj_kernel(x_ref, g_ref, w_ref, u_ref, dt_ref):
    xn = _rms(x_ref[...], g_ref[...], D_MODEL).astype(BF16)
    for n0 in range(0, U_END, 512):
        n1 = min(n0 + 512, U_END)
        u_ref[:, n0:n1] = jnp.dot(xn, w_ref[:, n0:n1], preferred_element_type=F32).astype(BF16)
    dt_ref[...] = jnp.dot(xn, w_ref[:, U_END:U_END + SSD_WIDTH], preferred_element_type=F32)


def _in_proj(h2d, g, w):
    rows = h2d.shape[0]
    tm = _pick(rows, (512, 256))
    return pl.pallas_call(
        _in_proj_kernel,
        grid=(rows // tm,),
        in_specs=[
            pl.BlockSpec((tm, D_MODEL), lambda i: (i, 0)),
            pl.BlockSpec((1, D_MODEL), lambda i: (0, 0)),
            pl.BlockSpec(w.shape, lambda i: (0, 0), pipeline_mode=pl.Buffered(1)),
        ],
        out_specs=[
            pl.BlockSpec((tm, U_END), lambda i: (i, 0)),
            pl.BlockSpec((tm, SSD_WIDTH), lambda i: (i, 0)),
        ],
        out_shape=[
            jax.ShapeDtypeStruct((rows, U_END), BF16),
            jax.ShapeDtypeStruct((rows, SSD_WIDTH), F32),
        ],
        compiler_params=_params("parallel"),
        name="in_proj",
    )(h2d, g, w)


def _cumsum_rows(x):
    n = x.shape[0]
    row = lax.broadcasted_iota(jnp.int32, x.shape, 0)
    s = 1
    while s < n:
        x = x + jnp.where(row >= s, pltpu.roll(x, s, 0), 0.0)
        s *= 2
    return x


def _ssd_kernel(z_ref, xbc_ref, dt_ref, cw_ref, cb_ref, dtb_ref, alog_ref, d_ref, g_ref,
                y_ref, ext_ref, st_ref):
    T = SSD_CHUNK

    @pl.when(pl.program_id(1) == 0)
    def _():
        ext_ref[0:8, :] = jnp.zeros((8, SSD_XBC), F32)
        st_ref[...] = jnp.zeros(st_ref.shape, F32)

    ext_ref[8:8 + T, :] = xbc_ref[0].astype(F32)
    cw = cw_ref[...]
    conv = cb_ref[...] + ext_ref[8:8 + T, :] * cw[3:4]
    for k in range(3):
        conv = conv + ext_ref[5 + k:5 + k + T, :] * cw[k:k + 1]
    ext_ref[0:8, :] = ext_ref[T:T + 8, :]
    xc = _silu(conv)
    xs = xc[:, :SSD_WIDTH]
    bm = xc[:, SSD_WIDTH:SSD_WIDTH + 2 * SSD_STATE]
    cm = xc[:, SSD_WIDTH + 2 * SSD_STATE:]

    dt = _softplus(dt_ref[0] + dtb_ref[...])
    acs = _cumsum_rows(dt * (-jnp.exp(alog_ref[...])))
    acs_last = acs[T - 1:T, :]
    x_dt = xs * dt
    xb = x_dt.astype(BF16)
    xd = (x_dt * jnp.exp(acs_last - acs)).astype(BF16)
    eacs = jnp.exp(acs)
    acs_t = acs.T

    tril = (lax.broadcasted_iota(jnp.int32, (T, T), 0) >= lax.broadcasted_iota(jnp.int32, (T, T), 1))
    low_half = lax.broadcasted_iota(jnp.int32, (T, LANES), 1) < SSD_HEAD_DIM
    y_groups = []
    for g in range(2):
        bg = bm[:, g * SSD_STATE:(g + 1) * SSD_STATE]
        cg = cm[:, g * SSD_STATE:(g + 1) * SSD_STATE].astype(BF16)
        cb = lax.dot_general(cg, bg.astype(BF16), NT_DIMS, preferred_element_type=F32)
        pair_out = []
        for pr in range(2):
            lane0 = g * 256 + pr * LANES
            xp = xb[:, lane0:lane0 + LANES]
            res = []
            for hh in range(2):
                hl = lane0 + hh * SSD_HEAD_DIM
                seg = acs[:, hl:hl + 1] - acs_t[hl:hl + 1, :]
                decay = jnp.where(tril, jnp.exp(seg), 0.0)
                res.append(jnp.dot((cb * decay).astype(BF16), xp, preferred_element_type=F32))
            pair_out.append(jnp.where(low_half, res[0], res[1]))
        y_diag = jnp.concatenate(pair_out, axis=1)
        hg = st_ref[g]
        y_off = jnp.dot(cg, hg.astype(BF16), preferred_element_type=F32) * eacs[:, g * 256:(g + 1) * 256]
        st_ref[g] = hg * jnp.exp(acs_last[:, g * 256:(g + 1) * 256]) + jnp.dot(
            bg.T.astype(BF16), xd[:, g * 256:(g + 1) * 256], preferred_element_type=F32)
        y_groups.append(y_diag + y_off)
    y = jnp.concatenate(y_groups, axis=1) + xs * d_ref[...]
    y = y * _silu(z_ref[0].astype(F32))
    y_ref[0] = _rms(y, g_ref[...], SSD_WIDTH).astype(BF16)


def _ssd(u3, dtx3, cw, cb, dtb, alog, dskip, g):
    bsz, lp, _ = u3.shape
    T = SSD_CHUNK
    vec = lambda n: pl.BlockSpec((1, n), lambda b, c: (0, 0))
    return pl.pallas_call(
        _ssd_kernel,
        grid=(bsz, lp // T),
        in_specs=[
            pl.BlockSpec((1, T, SSD_WIDTH), lambda b, c: (b, c, U_Z // SSD_WIDTH)),
            pl.BlockSpec((1, T, SSD_XBC), lambda b, c: (b, c, U_XBC // SSD_XBC)),
            pl.BlockSpec((1, T, SSD_WIDTH), lambda b, c: (b, c, 0)),
            pl.BlockSpec((4, SSD_XBC), lambda b, c: (0, 0)),
            vec(SSD_XBC), vec(SSD_WIDTH), vec(SSD_WIDTH), vec(SSD_WIDTH), vec(SSD_WIDTH),
        ],
        out_specs=pl.BlockSpec((1, T, SSD_WIDTH), lambda b, c: (b, c, 0)),
        out_shape=jax.ShapeDtypeStruct((bsz, lp, SSD_WIDTH), BF16),
        scratch_shapes=[pltpu.VMEM((T + 8, SSD_XBC), F32), pltpu.VMEM((2, SSD_STATE, 256), F32)],
        compiler_params=_params("parallel", "arbitrary"),
        name="ssd_scan",
    )(u3, u3, dtx3, cw, cb, dtb, alog, dskip, g)


def _key_start(qi, n, tq, lp):
    blk = (tq // ATT_SUB) * (qi + 1) - 1 - n
    return pl.multiple_of(jnp.clip(blk, 0, lp // ATT_SUB - 1) * ATT_SUB, ATT_SUB)


def _causal_masks(tq, strict):
    key_i = lax.broadcasted_iota(jnp.int32, (ATT_SUB, tq), 0)
    qry_i = lax.broadcasted_iota(jnp.int32, (ATT_SUB, tq), 1)
    nd = tq // ATT_SUB
    offs = [(nd - 1 - n) * ATT_SUB for n in range(nd)]
    return [(key_i + o < qry_i) if strict else (key_i + o <= qry_i) for o in offs]


def _head_queries(q, heads):
    lane = lax.broadcasted_iota(jnp.int32, (q.shape[0], LANES), 1)
    out = []
    for h in range(heads):
        pair = q[:, (h // 2) * LANES:(h // 2 + 1) * LANES].astype(F32)
        lo = 64 * (h % 2)
        out.append(jnp.where((lane >= lo) & (lane < lo + 64), pair, 0.0).astype(BF16))
    return out


def _pipeline3(nblk, ndiag, heads, stage1, stage2, stage3):
    diag1 = 1 if ndiag > 1 else None
    stage1(0, 0, 0)
    stage2(0, 0)
    stage1(1, 1, diag1)
    stage3(0, 0, True)
    if ndiag > 1:
        stage2(1, 1)
    else:
        stage2(1, None, nblk >= 2)
    stage1(2, 0)

    def step(n, par):
        for h in range(heads):
            stage2(1 - par, None, None, (h,))
            stage1(n + 2, par, None, (h,))
            stage3(n, par, False, (h,))

    def group(n0, count):
        for j in range(count):
            step(n0 + j, 1 - j % 2)

    steady = jnp.maximum(nblk - 3, 0)
    whole = steady // PIPE_UNROLL

    def body(i, c):
        group(1 + PIPE_UNROLL * i, PIPE_UNROLL)
        return c

    lax.fori_loop(0, whole, body, 0)
    rest = steady - whole * PIPE_UNROLL
    size = PIPE_UNROLL // 2
    while size >= 1:
        done = rest - rest % (2 * size)

        @pl.when(rest % (2 * size) >= size)
        def _(size=size, done=done):
            group(1 + whole * PIPE_UNROLL + done, size)

        size //= 2

    @pl.when(nblk >= 3)
    def _():
        par = (nblk - 2) % 2
        stage3(nblk - 2, par, False)
        stage2(1 - par, None)
        stage3(nblk - 1, 1 - par, False)

    @pl.when(nblk == 2)
    def _():
        stage3(1, 1, False)


def _sb_kernel(q_ref, k_ref, vt_ref, lt_ref, o_ref, s_scr, w_scr, tot_scr, acc_scr, r_scr, *, tq, lp):
    qi = pl.program_id(1)
    ndiag = tq // ATT_SUB
    nblk = ndiag * (qi + 1)
    qms = _head_queries(q_ref[0], SB_HEADS)
    later_t = lt_ref[...]
    masks = _causal_masks(tq, strict=True)
    pair = lambda h: slice((h // 2) * LANES, (h // 2 + 1) * LANES)
    rows = lambda h: slice(h * SB_HEAD_DIM, (h + 1) * SB_HEAD_DIM)

    all_heads = tuple(range(SB_HEADS))

    def stage1(n, par, diag=None, heads=all_heads):
        ks = _key_start(qi, n, tq, lp)
        for h in heads:
            s_scr[par, h] = lax.dot_general(k_ref[0, pl.ds(ks, ATT_SUB), pair(h)], qms[h], NT_DIMS,
                                            preferred_element_type=F32)

    def stage2(par, diag, exists=None, heads=all_heads):
        for h in heads:
            s = s_scr[par, h]
            sp = _softplus(s)
            if diag is not None:
                sp = jnp.where(masks[diag], sp, 0.0)
            loc = jnp.dot(later_t, sp.astype(BF16), preferred_element_type=F32)
            w = jnp.exp(s - loc)
            if diag is not None:
                w = jnp.where(masks[diag], w, 0.0)
            w_scr[par, h] = w.astype(BF16)
            tot_scr[par, h] = loc[0:1, :]

    def stage3(n, par, first, heads=all_heads):
        ks = _key_start(qi, n, tq, lp)
        for h in heads:
            pv = jnp.dot(vt_ref[0, rows(h), pl.ds(ks, ATT_SUB)], w_scr[par, h], preferred_element_type=F32)
            if first:
                acc_scr[rows(h), :] = pv
                r_scr[h] = -tot_scr[par, h]
            else:
                r = r_scr[h]
                acc_scr[rows(h), :] = acc_scr[rows(h), :] + jnp.exp(r) * pv
                r_scr[h] = r - tot_scr[par, h]

    _pipeline3(nblk, ndiag, SB_HEADS, stage1, stage2, stage3)
    o_ref[0] = acc_scr[...].T.astype(BF16)


def _sb_attention(u3, vt, later_t):
    bsz, lp, _ = u3.shape
    tq = ATT_TQ
    return pl.pallas_call(
        functools.partial(_sb_kernel, tq=tq, lp=lp),
        grid=(bsz, pl.cdiv(lp, tq)),
        in_specs=[
            pl.BlockSpec((1, tq, SB_WIDTH), lambda b, i: (b, i, U_QSB // SB_WIDTH)),
            pl.BlockSpec((1, lp, SB_WIDTH), lambda b, i: (b, 0, U_KSB // SB_WIDTH)),
            pl.BlockSpec((1, SB_WIDTH, lp), lambda b, i: (b, 0, 0)),
            pl.BlockSpec((ATT_SUB, ATT_SUB), lambda b, i: (0, 0)),
        ],
        out_specs=pl.BlockSpec((1, tq, SB_WIDTH), lambda b, i: (b, i, 0)),
        out_shape=jax.ShapeDtypeStruct((bsz, lp, SB_WIDTH), BF16),
        scratch_shapes=[pltpu.VMEM((2, SB_HEADS, ATT_SUB, tq), F32), pltpu.VMEM((2, SB_HEADS, ATT_SUB, tq), BF16),
                        pltpu.VMEM((2, SB_HEADS, 1, tq), F32), pltpu.VMEM((SB_WIDTH, tq), F32),
                        pltpu.VMEM((SB_HEADS, 1, tq), F32)],
        compiler_params=_params("parallel", "arbitrary"),
        name="sb_attention",
    )(u3, u3, vt, later_t)


def _mla_proj_kernel(qa_ref, ckv_ref, kr1_ref, kr2_ref, cos_ref, sin_ref, gq_ref, gkv_ref,
                     wq_ref, wqr_ref, wk_ref, wvt_ref, qc_ref, kc_ref, vt_ref):
    cos = cos_ref[...]
    sin = sin_ref[...]
    cos4 = jnp.concatenate([cos] * MLA_HEADS, axis=1)
    sin4 = jnp.concatenate([sin] * MLA_HEADS, axis=1)
    qn = _rms(qa_ref[...].astype(F32), gq_ref[...], MLA_Q_RANK).astype(BF16)
    q1 = jnp.dot(qn, wq_ref[...], preferred_element_type=F32)
    q2 = jnp.dot(qn, wqr_ref[...], preferred_element_type=F32)
    scale = (MLA_NOPE + MLA_ROPE) ** -0.5 * LOG2E
    qc_ref[...] = ((q1 * cos4 + q2 * sin4) * scale).astype(BF16)
    cn = _rms(ckv_ref[...].astype(F32), gkv_ref[...], MLA_KV_RANK).astype(BF16)
    kn = jnp.dot(cn, wk_ref[...], preferred_element_type=F32)
    krope = kr1_ref[...].astype(F32) * cos + kr2_ref[...].astype(F32) * sin
    kc_ref[...] = (kn + jnp.concatenate([krope] * MLA_HEADS, axis=1)).astype(BF16)
    vt_ref[0] = lax.dot_general(wvt_ref[...], cn, NT_DIMS, preferred_element_type=F32).astype(BF16)


def _mla_proj(u2d, cosx, sinx, gq, gkv, wq, wqr, wk, wvt, bsz, lp):
    rows = u2d.shape[0]
    tm = _pick(lp, (768, 512, 256))
    nper = lp // tm
    const = lambda a: pl.BlockSpec(a.shape, lambda i: (0, 0))
    return pl.pallas_call(
        _mla_proj_kernel,
        grid=(rows // tm,),
        in_specs=[
            pl.BlockSpec((tm, 256), lambda i: (i, U_QA // 256)),
            pl.BlockSpec((tm, LANES), lambda i: (i, U_CKV // LANES)),
            pl.BlockSpec((tm, LANES), lambda i: (i, U_KR1 // LANES)),
            pl.BlockSpec((tm, LANES), lambda i: (i, U_KR2 // LANES)),
            pl.BlockSpec((tm, LANES), lambda i: (i % nper, 0)),
            pl.BlockSpec((tm, LANES), lambda i: (i % nper, 0)),
            const(gq), const(gkv), const(wq), const(wqr), const(wk), const(wvt),
        ],
        out_specs=[
            pl.BlockSpec((tm, MLA_HEADS * LANES), lambda i: (i, 0)),
            pl.BlockSpec((tm, MLA_HEADS * LANES), lambda i: (i, 0)),
            pl.BlockSpec((1, 256, tm), lambda i: (i // nper, 0, i % nper)),
        ],
        out_shape=[
            jax.ShapeDtypeStruct((rows, MLA_HEADS * LANES), BF16),
            jax.ShapeDtypeStruct((rows, MLA_HEADS * LANES), BF16),
            jax.ShapeDtypeStruct((bsz, 256, lp), BF16),
        ],
        compiler_params=_params("parallel"),
        name="mla_proj",
    )(u2d, u2d, u2d, u2d, cosx, sinx, gq, gkv, wq, wqr, wk, wvt)


def _mla_kernel(q_ref, k_ref, vt_ref, o_ref, s_scr, cmax_scr, p_scr, alpha_scr, acc_scr, m_scr, l_scr,
                *, tq, lp):
    qi = pl.program_id(1)
    ndiag = tq // ATT_SUB
    nblk = ndiag * (qi + 1)
    qs = [q_ref[0, :, h * LANES:(h + 1) * LANES] for h in range(MLA_HEADS)]
    masks = _causal_masks(tq, strict=False)
    rows = lambda h: slice(h * MLA_V, (h + 1) * MLA_V)

    all_heads = tuple(range(MLA_HEADS))

    def stage1(n, par, diag=None, heads=all_heads):
        ks = _key_start(qi, n, tq, lp)
        for h in heads:
            s = lax.dot_general(k_ref[0, pl.ds(ks, ATT_SUB), h * LANES:(h + 1) * LANES], qs[h],
                                NT_DIMS, preferred_element_type=F32)
            if diag is not None:
                s = jnp.where(masks[diag], s, -1e30)
            s_scr[par, h] = s
            cmax_scr[par, h] = jnp.max(s, axis=0, keepdims=True)

    def stage2(par, diag, exists=None, heads=all_heads):
        first = diag == 0
        for h in heads:
            if first:
                m_new = cmax_scr[par, h]
            else:
                m_old = m_scr[h]
                m_new = jnp.maximum(m_old, cmax_scr[par, h])
            p = jnp.exp2(s_scr[par, h] - m_new)
            p_scr[par, h] = p.astype(BF16)
            l_new = jnp.sum(p, axis=0, keepdims=True)
            if not first:
                alpha = jnp.exp2(m_old - m_new)
                alpha_scr[par, h] = alpha
                l_new = alpha * l_scr[h] + l_new
                if exists is not None:
                    l_new = jnp.where(exists, l_new, l_scr[h])
                    m_new = jnp.where(exists, m_new, m_old)
            l_scr[h] = l_new
            m_scr[h] = m_new

    def stage3(n, par, first, heads=all_heads):
        ks = _key_start(qi, n, tq, lp)
        for h in heads:
            pv = jnp.dot(vt_ref[0, rows(h), pl.ds(ks, ATT_SUB)], p_scr[par, h], preferred_element_type=F32)
            if first:
                acc_scr[rows(h), :] = pv
            else:
                acc_scr[rows(h), :] = alpha_scr[par, h] * acc_scr[rows(h), :] + pv

    _pipeline3(nblk, ndiag, MLA_HEADS, stage1, stage2, stage3)
    for h in range(MLA_HEADS):
        acc_scr[rows(h), :] = acc_scr[rows(h), :] / l_scr[h]
    o_ref[0] = acc_scr[...].T.astype(BF16)


def _mla_attention(qc3, kc3, vt):
    bsz, lp, width = qc3.shape
    tq = ATT_TQ
    out_w = MLA_HEADS * MLA_V
    return pl.pallas_call(
        functools.partial(_mla_kernel, tq=tq, lp=lp),
        grid=(bsz, pl.cdiv(lp, tq)),
        in_specs=[
            pl.BlockSpec((1, tq, width), lambda b, i: (b, i, 0)),
            pl.BlockSpec((1, lp, width), lambda b, i: (b, 0, 0)),
            pl.BlockSpec((1, out_w, lp), lambda b, i: (b, 0, 0)),
        ],
        out_specs=pl.BlockSpec((1, tq, out_w), lambda b, i: (b, i, 0)),
        out_shape=jax.ShapeDtypeStruct((bsz, lp, out_w), BF16),
        scratch_shapes=[pltpu.VMEM((2, MLA_HEADS, ATT_SUB, tq), F32), pltpu.VMEM((2, MLA_HEADS, 1, tq), F32),
                        pltpu.VMEM((2, MLA_HEADS, ATT_SUB, tq), BF16),
                        pltpu.VMEM((2, MLA_HEADS, 1, tq), F32), pltpu.VMEM((out_w, tq), F32),
                        pltpu.VMEM((MLA_HEADS, 1, tq), F32), pltpu.VMEM((MLA_HEADS, 1, tq), F32)],
        compiler_params=_params("parallel", "arbitrary"),
        name="mla_attention",
    )(qc3, kc3, vt)


def _out_ffn_kernel(h_ref, yssd_ref, osb_ref, omla_ref, gsb_ref, gmla_ref, wout_ref, gffn_ref,
                    wup_ref, cw_ref, cb_ref, wdown_ref, o_ref, halo_ref, ext_ref, act_ref, *, tm):
    @pl.when(pl.program_id(1) == 0)
    def _():
        halo_ref[...] = jnp.zeros(halo_ref.shape, F32)

    ysb = _rms(osb_ref[0].astype(F32), gsb_ref[...], SB_WIDTH).astype(BF16)
    ymla = _rms(omla_ref[0].astype(F32), gmla_ref[...], 256).astype(BF16)
    mix = jnp.dot(yssd_ref[0], wout_ref[0:512, :], preferred_element_type=F32)
    mix = mix + jnp.dot(ysb, wout_ref[512:768, :], preferred_element_type=F32)
    mix = mix + jnp.dot(ymla, wout_ref[768:1024, :], preferred_element_type=F32)
    h1 = h_ref[0] + mix
    o_ref[0] = h1
    xn = _rms(h1, gffn_ref[...], D_MODEL).astype(BF16)

    w2 = 2 * FF_CHUNK
    for c in range(D_FF // FF_CHUNK):
        cols = slice(c * w2, (c + 1) * w2)
        ext_ref[0:8, :] = halo_ref[:, cols]
        ext_ref[8:8 + tm, :] = jnp.dot(xn, wup_ref[:, cols], preferred_element_type=F32)
        halo_ref[:, cols] = ext_ref[tm:tm + 8, :]
        cw = cw_ref[:, cols]
        conv = cb_ref[:, cols] + ext_ref[8:8 + tm, :] * cw[2:3]
        conv = conv + ext_ref[7:7 + tm, :] * cw[1:2]
        conv = conv + ext_ref[6:6 + tm, :] * cw[0:1]
        act = _silu(conv[:, :FF_CHUNK]) * conv[:, FF_CHUNK:]
        act_ref[:, c * FF_CHUNK:(c + 1) * FF_CHUNK] = act.astype(BF16)
    o_ref[0] = o_ref[0] + jnp.dot(act_ref[...], wdown_ref[...], preferred_element_type=F32)


def _out_ffn(h3, yssd, osb, omla, gsb, gmla, wout, gffn, wup, cw, cb, wdown):
    bsz, lp, _ = h3.shape
    tm = _pick(lp, (768, 512, 256))
    const = lambda a: pl.BlockSpec(a.shape, lambda b, t: (0, 0), pipeline_mode=pl.Buffered(1))
    rowblk = lambda n: pl.BlockSpec((1, tm, n), lambda b, t: (b, t, 0))
    return pl.pallas_call(
        functools.partial(_out_ffn_kernel, tm=tm),
        grid=(bsz, lp // tm),
        in_specs=[rowblk(D_MODEL), rowblk(SSD_WIDTH), rowblk(256), rowblk(256),
                  const(gsb), const(gmla), const(wout), const(gffn),
                  const(wup), const(cw), const(cb), const(wdown)],
        out_specs=rowblk(D_MODEL),
        out_shape=jax.ShapeDtypeStruct(h3.shape, F32),
        scratch_shapes=[pltpu.VMEM((8, 2 * D_FF), F32),
                        pltpu.VMEM((tm + 8, 2 * FF_CHUNK), F32),
                        pltpu.VMEM((tm, D_FF), BF16)],
        compiler_params=_params("parallel", "arbitrary"),
        name="out_ffn",
    )(h3, yssd, osb, omla, gsb, gmla, wout, gffn, wup, cw, cb, wdown)


def _final_kernel(a_ref, t_ref, g_ref, o_ref, *, tm):
    g = g_ref[...]
    o_ref[0, 0:tm - N_META, :] = _rms(a_ref[0, N_META:tm, :], g, D_MODEL)
    o_ref[0, tm - N_META:tm, :] = _rms(t_ref[0], g, D_MODEL)


def _final_norm(h3, g, seq):
    bsz = h3.shape[0]
    tm = _pick(seq, (512, 128))
    return pl.pallas_call(
        functools.partial(_final_kernel, tm=tm),
        grid=(bsz, seq // tm),
        in_specs=[
            pl.BlockSpec((1, tm, D_MODEL), lambda b, i: (b, i, 0)),
            pl.BlockSpec((1, N_META, D_MODEL), lambda b, i: (b, (i + 1) * (tm // N_META), 0)),
            pl.BlockSpec((1, D_MODEL), lambda b, i: (0, 0)),
        ],
        out_specs=pl.BlockSpec((1, tm, D_MODEL), lambda b, i: (b, i, 0)),
        out_shape=jax.ShapeDtypeStruct((bsz, seq, D_MODEL), F32),
        compiler_params=_params("parallel", "parallel"),
        name="final_norm",
    )(h3, h3, g)


def _rot_cols(w):
    half = w.shape[1] // 2
    return jnp.concatenate([-w[:, half:], w[:, :half]], axis=1)


def _prep_in_proj(w_in):
    zeros = lambda n: jnp.zeros((D_MODEL, n), F32)
    cuts = (512, 1536, 1544, 1800, 2056, 2312, 2504, 2632)
    z, xbc, dt, q_sb, k_sb, v_sb, q_a, c_kv, k_r = jnp.split(w_in, cuts, axis=1)
    kr1 = jnp.concatenate([zeros(MLA_NOPE), k_r, zeros(LANES - MLA_NOPE - MLA_ROPE)], axis=1)
    kr2 = jnp.concatenate([zeros(MLA_NOPE), _rot_cols(k_r), zeros(LANES - MLA_NOPE - MLA_ROPE)], axis=1)
    w = jnp.concatenate([xbc, z, q_sb * (64 ** -0.5), k_sb, v_sb, q_a, zeros(256 - MLA_Q_RANK), c_kv,
                         kr1, kr2, jnp.repeat(dt, SSD_HEAD_DIM, axis=1)], axis=1)
    return w.astype(BF16)


def _prep_mla(w_uq, w_ukv):
    qh = w_uq.reshape(MLA_Q_RANK, MLA_HEADS, MLA_NOPE + MLA_ROPE)
    pad = jnp.zeros((MLA_Q_RANK, MLA_HEADS, LANES - MLA_NOPE - MLA_ROPE), F32)
    nope0 = jnp.zeros((MLA_Q_RANK, MLA_HEADS, MLA_NOPE), F32)
    rope = qh[..., MLA_NOPE:]
    rope_rot = jnp.concatenate([-rope[..., MLA_ROPE // 2:], rope[..., :MLA_ROPE // 2]], axis=-1)
    wq = jnp.concatenate([qh, pad], axis=-1).reshape(MLA_Q_RANK, MLA_HEADS * LANES)
    wqr = jnp.concatenate([nope0, rope_rot, pad], axis=-1).reshape(MLA_Q_RANK, MLA_HEADS * LANES)
    rowpad = jnp.zeros((256 - MLA_Q_RANK, MLA_HEADS * LANES), F32)
    wq = jnp.concatenate([wq, rowpad], axis=0).astype(BF16)
    wqr = jnp.concatenate([wqr, rowpad], axis=0).astype(BF16)
    kvh = w_ukv.reshape(MLA_KV_RANK, MLA_HEADS, MLA_NOPE + 64)
    kpad = jnp.zeros((MLA_KV_RANK, MLA_HEADS, LANES - MLA_NOPE), F32)
    wk = jnp.concatenate([kvh[..., :MLA_NOPE], kpad], axis=-1).reshape(MLA_KV_RANK, MLA_HEADS * LANES)
    wvt = kvh[..., MLA_NOPE:].reshape(MLA_KV_RANK, MLA_HEADS * 64).T
    return wq, wqr, wk.astype(BF16), wvt.astype(BF16)


def _interleave_ff(a):
    lead = a.shape[:-1]
    g = a[..., :D_FF].reshape(lead + (D_FF // FF_CHUNK, FF_CHUNK))
    v = a[..., D_FF:].reshape(lead + (D_FF // FF_CHUNK, FF_CHUNK))
    return jnp.concatenate([g, v], axis=-1).reshape(lead + (2 * D_FF,))


def _rope_tables(lp):
    pos = jnp.arange(lp, dtype=F32)
    inv = 1.0 / (ROPE_BASE ** (jnp.arange(0, MLA_ROPE, 2, dtype=F32) / MLA_ROPE))
    ang = pos[:, None] * inv[None, :]
    ang = jnp.concatenate([ang, ang], axis=-1)
    ones = jnp.ones((lp, MLA_NOPE), F32)
    tail = LANES - MLA_NOPE - MLA_ROPE
    cosx = jnp.concatenate([ones, jnp.cos(ang), jnp.ones((lp, tail), F32)], axis=1)
    sinx = jnp.concatenate([0 * ones, jnp.sin(ang), jnp.zeros((lp, tail), F32)], axis=1)
    return cosx, sinx


def kernel(x, meta_tokens, norm_mix_g, w_in, ssd_conv_w, ssd_conv_b, ssd_dt_bias, ssd_a_log, ssd_d, ssd_norm_g, sb_norm_g, mla_q_norm_g, mla_kv_norm_g, mla_w_uq, mla_w_ukv, mla_norm_g, w_out, norm_ffn_g, ffn_w_up, ffn_conv_w, ffn_conv_b, ffn_w_down, final_norm_g):
    bsz, seq, _ = x.shape
    length = N_META + seq
    lp = -(-length // SEQ_ALIGN) * SEQ_ALIGN
    depth = w_in.shape[0]

    meta = jnp.broadcast_to(meta_tokens[None].astype(x.dtype), (bsz, N_META, D_MODEL))
    h = jnp.concatenate([meta, x, jnp.zeros((bsz, lp - length, D_MODEL), x.dtype)], axis=1)

    cosx, sinx = _rope_tables(lp)
    later_t = (lax.broadcasted_iota(jnp.int32, (ATT_SUB, ATT_SUB), 1) >=
               lax.broadcasted_iota(jnp.int32, (ATT_SUB, ATT_SUB), 0)).astype(BF16)
    row = lambda v: v.reshape(1, -1)
    rep = lambda v: jnp.repeat(v, SSD_HEAD_DIM).reshape(1, -1)

    for l in range(depth):
        u, dtx = _in_proj(h.reshape(bsz * lp, D_MODEL), row(norm_mix_g[l]), _prep_in_proj(w_in[l]))
        u3 = u.reshape(bsz, lp, U_END)
        y_ssd = _ssd(u3, dtx.reshape(bsz, lp, SSD_WIDTH), ssd_conv_w[l], row(ssd_conv_b[l]),
                     rep(ssd_dt_bias[l]), rep(ssd_a_log[l]), rep(ssd_d[l]), row(ssd_norm_g[l]))
        vt_sb = jnp.swapaxes(u3[:, :, U_VSB:U_VSB + SB_WIDTH], 1, 2)
        o_sb = _sb_attention(u3, vt_sb, later_t)
        wq, wqr, wk, wvt = _prep_mla(mla_w_uq[l], mla_w_ukv[l])
        gq = jnp.concatenate([mla_q_norm_g[l], jnp.zeros((256 - MLA_Q_RANK,), F32)]).reshape(1, -1)
        qc, kc, vt_mla = _mla_proj(u, cosx, sinx, gq, row(mla_kv_norm_g[l]), wq, wqr, wk, wvt, bsz, lp)
        o_mla = _mla_attention(qc.reshape(bsz, lp, -1), kc.reshape(bsz, lp, -1), vt_mla)
        h = _out_ffn(h, y_ssd, o_sb, o_mla, row(sb_norm_g[l]), row(mla_norm_g[l]),
                     w_out[l].astype(BF16), row(norm_ffn_g[l]),
                     _interleave_ff(ffn_w_up[l]).astype(BF16), _interleave_ff(ffn_conv_w[l]),
                     row(_interleave_ff(ffn_conv_b[l])), ffn_w_down[l].astype(BF16))

    return _final_norm(h, row(final_norm_g), seq)
```

```python
import functools
import math

import jax
import jax.numpy as jnp
from jax import lax
from jax.experimental import pallas as pl
from jax.experimental.pallas import tpu as pltpu

F32 = jnp.float32
BF16 = jnp.bfloat16

D_MODEL = 1024
N_META = 16
EPS = 1e-6
SSD_HEADS = 8
SSD_HEAD_DIM = 64
SSD_WIDTH = 512
SSD_STATE = 128
SSD_XBC = 1024
SSD_CHUNK = 128
SB_HEADS = 4
SB_HEAD_DIM = 64
SB_WIDTH = 256
MLA_HEADS = 4
MLA_V = 64
MLA_NOPE = 64
MLA_ROPE = 32
MLA_Q_RANK = 192
MLA_KV_RANK = 128
ROPE_BASE = 10000.0
D_FF = 2816
FF_CHUNK = 256

LANES = 128
ATT_SUB = 256
ATT_TQ = 256
PIPE_UNROLL = 4
SEQ_ALIGN = ATT_TQ
VMEM_LIMIT = 56 * 1024 * 1024

U_XBC, U_Z, U_QSB, U_KSB, U_VSB, U_QA, U_CKV, U_KR1, U_KR2, U_END = (
    0, 1024, 1536, 1792, 2048, 2304, 2560, 2688, 2816, 2944)

NT_DIMS = (((1,), (1,)), ((), ()))
LOG2E = 1.4426950408889634


def _pick(n, candidates):
    for c in candidates:
        if n % c == 0:
            return c
    raise ValueError(f"no tile in {candidates} divides {n}")


def _softplus(x):
    return jnp.maximum(x, 0.0) + jnp.log(1.0 + jnp.exp2(jnp.abs(x) * (-LOG2E)))


def _silu(x):
    return x * jax.nn.sigmoid(x)


def _rms(x, g, n):
    ms = jnp.sum(x * x, axis=-1, keepdims=True) * (1.0 / n)
    return (x * lax.rsqrt(ms + EPS)) * g


def _params(*sem):
    return pltpu.CompilerParams(dimension_semantics=sem, vmem_limit_bytes=VMEM_LIMIT)


def _in_proj_kernel(x_ref, g_ref, w_ref, u_ref, dt_ref):
    xn = _rms(x_ref[...], g_ref[...], D_MODEL).astype(BF16)
    for n0 in range(0, U_END, 512):
        n1 = min(n0 + 512, U_END)
        u_ref[:, n0:n1] = jnp.dot(xn, w_ref[:, n0:n1], preferred_element_type=F32).astype(BF16)
    dt_ref[...] = jnp.dot(xn, w_ref[:, U_END:U_END + SSD_WIDTH], preferred_element_type=F32)


def _in_proj(h2d, g, w):
    rows = h2d.shape[0]
    tm = _pick(rows, (512, 256))
    return pl.pallas_call(
        _in_proj_kernel,
        grid=(rows // tm,),
        in_specs=[
            pl.BlockSpec((tm, D_MODEL), lambda i: (i, 0)),
            pl.BlockSpec((1, D_MODEL), lambda i: (0, 0)),
            pl.BlockSpec(w.shape, lambda i: (0, 0), pipeline_mode=pl.Buffered(1)),
        ],
        out_specs=[
            pl.BlockSpec((tm, U_END), lambda i: (i, 0)),
            pl.BlockSpec((tm, SSD_WIDTH), lambda i: (i, 0)),
        ],
        out_shape=[
            jax.ShapeDtypeStruct((rows, U_END), BF16),
            jax.ShapeDtypeStruct((rows, SSD_WIDTH), F32),
        ],
        compiler_params=_params("parallel"),
        name="in_proj",
    )(h2d, g, w)


def _cumsum_rows(x):
    n = x.shape[0]
    row = lax.broadcasted_iota(jnp.int32, x.shape, 0)
    s = 1
    while s < n:
        x = x + jnp.where(row >= s, pltpu.roll(x, s, 0), 0.0)
        s *= 2
    return x


def _ssd_kernel(z_ref, xbc_ref, dt_ref, cw_ref, cb_ref, dtb_ref, alog_ref, d_ref, g_ref,
                y_ref, ext_ref, st_ref):
    T = SSD_CHUNK

    @pl.when(pl.program_id(1) == 0)
    def _():
        ext_ref[0:8, :] = jnp.zeros((8, SSD_XBC), F32)
        st_ref[...] = jnp.zeros(st_ref.shape, F32)

    ext_ref[8:8 + T, :] = xbc_ref[0].astype(F32)
    cw = cw_ref[...]
    conv = cb_ref[...] + ext_ref[8:8 + T, :] * cw[3:4]
    for k in range(3):
        conv = conv + ext_ref[5 + k:5 + k + T, :] * cw[k:k + 1]
    ext_ref[0:8, :] = ext_ref[T:T + 8, :]
    xc = _silu(conv)
    xs = xc[:, :SSD_WIDTH]
    bm = xc[:, SSD_WIDTH:SSD_WIDTH + 2 * SSD_STATE]
    cm = xc[:, SSD_WIDTH + 2 * SSD_STATE:]

    dt = _softplus(dt_ref[0] + dtb_ref[...])
    acs = _cumsum_rows(dt * (-jnp.exp(alog_ref[...])))
    acs_last = acs[T - 1:T, :]
    x_dt = xs * dt
    xb = x_dt.astype(BF16)
    xd = (x_dt * jnp.exp(acs_last - acs)).astype(BF16)
    eacs = jnp.exp(acs)
    acs_t = acs.T

    tril = (lax.broadcasted_iota(jnp.int32, (T, T), 0) >= lax.broadcasted_iota(jnp.int32, (T, T), 1))
    low_half = lax.broadcasted_iota(jnp.int32, (T, LANES), 1) < SSD_HEAD_DIM
    y_groups = []
    for g in range(2):
        bg = bm[:, g * SSD_STATE:(g + 1) * SSD_STATE]
        cg = cm[:, g * SSD_STATE:(g + 1) * SSD_STATE].astype(BF16)
        cb = lax.dot_general(cg, bg.astype(BF16), NT_DIMS, preferred_element_type=F32)
        pair_out = []
        for pr in range(2):
            lane0 = g * 256 + pr * LANES
            xp = xb[:, lane0:lane0 + LANES]
            res = []
            for hh in range(2):
                hl = lane0 + hh * SSD_HEAD_DIM
                seg = acs[:, hl:hl + 1] - acs_t[hl:hl + 1, :]
                decay = jnp.where(tril, jnp.exp(seg), 0.0)
                res.append(jnp.dot((cb * decay).astype(BF16), xp, preferred_element_type=F32))
            pair_out.append(jnp.where(low_half, res[0], res[1]))
        y_diag = jnp.concatenate(pair_out, axis=1)
        hg = st_ref[g]
        y_off = jnp.dot(cg, hg.astype(BF16), preferred_element_type=F32) * eacs[:, g * 256:(g + 1) * 256]
        st_ref[g] = hg * jnp.exp(acs_last[:, g * 256:(g + 1) * 256]) + jnp.dot(
            bg.T.astype(BF16), xd[:, g * 256:(g + 1) * 256], preferred_element_type=F32)
        y_groups.append(y_diag + y_off)
    y = jnp.concatenate(y_groups, axis=1) + xs * d_ref[...]
    y = y * _silu(z_ref[0].astype(F32))
    y_ref[0] = _rms(y, g_ref[...], SSD_WIDTH).astype(BF16)


def _ssd(u3, dtx3, cw, cb, dtb, alog, dskip, g):
    bsz, lp, _ = u3.shape
    T = SSD_CHUNK
    vec = lambda n: pl.BlockSpec((1, n), lambda b, c: (0, 0))
    return pl.pallas_call(
        _ssd_kernel,
        grid=(bsz, lp // T),
        in_specs=[
            pl.BlockSpec((1, T, SSD_WIDTH), lambda b, c: (b, c, U_Z // SSD_WIDTH)),
            pl.BlockSpec((1, T, SSD_XBC), lambda b, c: (b, c, U_XBC // SSD_XBC)),
            pl.BlockSpec((1, T, SSD_WIDTH), lambda b, c: (b, c, 0)),
            pl.BlockSpec((4, SSD_XBC), lambda b, c: (0, 0)),
            vec(SSD_XBC), vec(SSD_WIDTH), vec(SSD_WIDTH), vec(SSD_WIDTH), vec(SSD_WIDTH),
        ],
        out_specs=pl.BlockSpec((1, T, SSD_WIDTH), lambda b, c: (b, c, 0)),
        out_shape=jax.ShapeDtypeStruct((bsz, lp, SSD_WIDTH), BF16),
        scratch_shapes=[pltpu.VMEM((T + 8, SSD_XBC), F32), pltpu.VMEM((2, SSD_STATE, 256), F32)],
        compiler_params=_params("parallel", "arbitrary"),
        name="ssd_scan",
    )(u3, u3, dtx3, cw, cb, dtb, alog, dskip, g)


def _key_start(qi, n, tq, lp):
    blk = (tq // ATT_SUB) * (qi + 1) - 1 - n
    return pl.multiple_of(jnp.clip(blk, 0, lp // ATT_SUB - 1) * ATT_SUB, ATT_SUB)


def _causal_masks(tq, strict):
    key_i = lax.broadcasted_iota(jnp.int32, (ATT_SUB, tq), 0)
    qry_i = lax.broadcasted_iota(jnp.int32, (ATT_SUB, tq), 1)
    nd = tq // ATT_SUB
    offs = [(nd - 1 - n) * ATT_SUB for n in range(nd)]
    return [(key_i + o < qry_i) if strict else (key_i + o <= qry_i) for o in offs]


def _head_queries(q, heads):
    lane = lax.broadcasted_iota(jnp.int32, (q.shape[0], LANES), 1)
    out = []
    for h in range(heads):
        pair = q[:, (h // 2) * LANES:(h // 2 + 1) * LANES].astype(F32)
        lo = 64 * (h % 2)
        out.append(jnp.where((lane >= lo) & (lane < lo + 64), pair, 0.0).astype(BF16))
    return out


def _pipeline3(nblk, ndiag, heads, stage1, stage2, stage3):
    diag1 = 1 if ndiag > 1 else None
    for h in range(heads):
        hs = (h,)
        stage1(0, 0, 0, hs)
        stage2(0, 0, None, hs)
        stage1(1, 1, diag1, hs)
        stage3(0, 0, True, hs)
        if ndiag > 1:
            stage2(1, 1, None, hs)
        else:
            stage2(1, None, nblk >= 2, hs)
        stage1(2, 0, None, hs)

    def step(n, par):
        for h in range(heads):
            stage1(n + 2, par, None, (h,))
            stage3(n, par, False, (h,))
            stage2(1 - par, None, None, (h,))

    def group(n0, count):
        for j in range(count):
            step(n0 + j, 1 - j % 2)

    steady = jnp.maximum(nblk - 3, 0)
    whole = steady // PIPE_UNROLL

    def body(i, c):
        group(1 + PIPE_UNROLL * i, PIPE_UNROLL)
        return c

    lax.fori_loop(0, whole, body, 0)
    rest = steady - whole * PIPE_UNROLL
    size = PIPE_UNROLL // 2
    while size >= 1:
        done = rest - rest % (2 * size)

        @pl.when(rest % (2 * size) >= size)
        def _(size=size, done=done):
            group(1 + whole * PIPE_UNROLL + done, size)

        size //= 2

    @pl.when(nblk >= 3)
    def _():
        par = (nblk - 2) % 2
        for h in range(heads):
            stage3(nblk - 2, par, False, (h,))
            stage2(1 - par, None, None, (h,))
            stage3(nblk - 1, 1 - par, False, (h,))

    @pl.when(nblk == 2)
    def _():
        stage3(1, 1, False)


def _sb_kernel(q_ref, k_ref, vt_ref, lt_ref, o_ref, s_scr, w_scr, tot_scr, acc_scr, r_scr, *, tq, lp):
    qi = pl.program_id(1)
    ndiag = tq // ATT_SUB
    nblk = ndiag * (qi + 1)
    qms = _head_queries(q_ref[0], SB_HEADS)
    later_t = lt_ref[...]
    masks = _causal_masks(tq, strict=True)
    pair = lambda h: slice((h // 2) * LANES, (h // 2 + 1) * LANES)
    rows = lambda h: slice(h * SB_HEAD_DIM, (h + 1) * SB_HEAD_DIM)

    all_heads = tuple(range(SB_HEADS))

    def stage1(n, par, diag=None, heads=all_heads):
        ks = _key_start(qi, n, tq, lp)
        for h in heads:
            s_scr[par, h] = lax.dot_general(k_ref[0, pl.ds(ks, ATT_SUB), pair(h)], qms[h], NT_DIMS,
                                            preferred_element_type=F32)

    def stage2(par, diag, exists=None, heads=all_heads):
        for h in heads:
            s = s_scr[par, h]
            sp = _softplus(s)
            if diag is not None:
                sp = jnp.where(masks[diag], sp, 0.0)
            loc = jnp.dot(later_t, sp.astype(BF16), preferred_element_type=F32)
            w = jnp.exp(s - loc)
            if diag is not None:
                w = jnp.where(masks[diag], w, 0.0)
            w_scr[par, h] = w.astype(BF16)
            tot_scr[par, h] = loc[0:1, :]

    def stage3(n, par, first, heads=all_heads):
        ks = _key_start(qi, n, tq, lp)
        for h in heads:
            pv = jnp.dot(vt_ref[0, rows(h), pl.ds(ks, ATT_SUB)], w_scr[par, h], preferred_element_type=F32)
            if first:
                acc_scr[rows(h), :] = pv
                r_scr[h] = -tot_scr[par, h]
            else:
                r = r_scr[h]
                acc_scr[rows(h), :] = acc_scr[rows(h), :] + jnp.exp(r) * pv
                r_scr[h] = r - tot_scr[par, h]

    _pipeline3(nblk, ndiag, SB_HEADS, stage1, stage2, stage3)
    o_ref[0] = acc_scr[...].T.astype(BF16)


def _sb_attention(u3, vt, later_t):
    bsz, lp, _ = u3.shape
    tq = ATT_TQ
    return pl.pallas_call(
        functools.partial(_sb_kernel, tq=tq, lp=lp),
        grid=(bsz, pl.cdiv(lp, tq)),
        in_specs=[
            pl.BlockSpec((1, tq, SB_WIDTH), lambda b, i: (b, i, U_QSB // SB_WIDTH)),
            pl.BlockSpec((1, lp, SB_WIDTH), lambda b, i: (b, 0, U_KSB // SB_WIDTH)),
            pl.BlockSpec((1, SB_WIDTH, lp), lambda b, i: (b, 0, 0)),
            pl.BlockSpec((ATT_SUB, ATT_SUB), lambda b, i: (0, 0)),
        ],
        out_specs=pl.BlockSpec((1, tq, SB_WIDTH), lambda b, i: (b, i, 0)),
        out_shape=jax.ShapeDtypeStruct((bsz, lp, SB_WIDTH), BF16),
        scratch_shapes=[pltpu.VMEM((2, SB_HEADS, ATT_SUB, tq), F32), pltpu.VMEM((2, SB_HEADS, ATT_SUB, tq), BF16),
                        pltpu.VMEM((2, SB_HEADS, 1, tq), F32), pltpu.VMEM((SB_WIDTH, tq), F32),
                        pltpu.VMEM((SB_HEADS, 1, tq), F32)],
        compiler_params=_params("parallel", "arbitrary"),
        name="sb_attention",
    )(u3, u3, vt, later_t)


def _mla_proj_kernel(qa_ref, ckv_ref, kr1_ref, kr2_ref, cos_ref, sin_ref, gq_ref, gkv_ref,
                     wq_ref, wqr_ref, wk_ref, wvt_ref, qc_ref, kc_ref, vt_ref):
    cos = cos_ref[...]
    sin = sin_ref[...]
    cos4 = jnp.concatenate([cos] * MLA_HEADS, axis=1)
    sin4 = jnp.concatenate([sin] * MLA_HEADS, axis=1)
    qn = _rms(qa_ref[...].astype(F32), gq_ref[...], MLA_Q_RANK).astype(BF16)
    q1 = jnp.dot(qn, wq_ref[...], preferred_element_type=F32)
    q2 = jnp.dot(qn, wqr_ref[...], preferred_element_type=F32)
    scale = (MLA_NOPE + MLA_ROPE) ** -0.5 * LOG2E
    qc_ref[...] = ((q1 * cos4 + q2 * sin4) * scale).astype(BF16)
    cn = _rms(ckv_ref[...].astype(F32), gkv_ref[...], MLA_KV_RANK).astype(BF16)
    kn = jnp.dot(cn, wk_ref[...], preferred_element_type=F32)
    krope = kr1_ref[...].astype(F32) * cos + kr2_ref[...].astype(F32) * sin
    kc_ref[...] = (kn + jnp.concatenate([krope] * MLA_HEADS, axis=1)).astype(BF16)
    vt_ref[0] = lax.dot_general(wvt_ref[...], cn, NT_DIMS, preferred_element_type=F32).astype(BF16)


def _mla_proj(u2d, cosx, sinx, gq, gkv, wq, wqr, wk, wvt, bsz, lp):
    rows = u2d.shape[0]
    tm = _pick(lp, (768, 512, 256))
    nper = lp // tm
    const = lambda a: pl.BlockSpec(a.shape, lambda i: (0, 0))
    return pl.pallas_call(
        _mla_proj_kernel,
        grid=(rows // tm,),
        in_specs=[
            pl.BlockSpec((tm, 256), lambda i: (i, U_QA // 256)),
            pl.BlockSpec((tm, LANES), lambda i: (i, U_CKV // LANES)),
            pl.BlockSpec((tm, LANES), lambda i: (i, U_KR1 // LANES)),
            pl.BlockSpec((tm, LANES), lambda i: (i, U_KR2 // LANES)),
            pl.BlockSpec((tm, LANES), lambda i: (i % nper, 0)),
            pl.BlockSpec((tm, LANES), lambda i: (i % nper, 0)),
            const(gq), const(gkv), const(wq), const(wqr), const(wk), const(wvt),
        ],
        out_specs=[
            pl.BlockSpec((tm, MLA_HEADS * LANES), lambda i: (i, 0)),
            pl.BlockSpec((tm, MLA_HEADS * LANES), lambda i: (i, 0)),
            pl.BlockSpec((1, 256, tm), lambda i: (i // nper, 0, i % nper)),
        ],
        out_shape=[
            jax.ShapeDtypeStruct((rows, MLA_HEADS * LANES), BF16),
            jax.ShapeDtypeStruct((rows, MLA_HEADS * LANES), BF16),
            jax.ShapeDtypeStruct((bsz, 256, lp), BF16),
        ],
        compiler_params=_params("parallel"),
        name="mla_proj",
    )(u2d, u2d, u2d, u2d, cosx, sinx, gq, gkv, wq, wqr, wk, wvt)


def _mla_kernel(q_ref, k_ref, vt_ref, o_ref, s_scr, cmax_scr, p_scr, alpha_scr, acc_scr, m_scr, l_scr,
                *, tq, lp):
    qi = pl.program_id(1)
    ndiag = tq // ATT_SUB
    nblk = ndiag * (qi + 1)
    qs = [q_ref[0, :, h * LANES:(h + 1) * LANES] for h in range(MLA_HEADS)]
    masks = _causal_masks(tq, strict=False)
    rows = lambda h: slice(h * MLA_V, (h + 1) * MLA_V)

    all_heads = tuple(range(MLA_HEADS))

    def stage1(n, par, diag=None, heads=all_heads):
        ks = _key_start(qi, n, tq, lp)
        for h in heads:
            s = lax.dot_general(k_ref[0, pl.ds(ks, ATT_SUB), h * LANES:(h + 1) * LANES], qs[h],
                                NT_DIMS, preferred_element_type=F32)
            if diag is not None:
                s = jnp.where(masks[diag], s, -1e30)
            s_scr[par, h] = s
            cmax_scr[par, h] = jnp.max(s, axis=0, keepdims=True)

    def stage2(par, diag, exists=None, heads=all_heads):
        first = diag == 0
        for h in heads:
            if first:
                m_new = cmax_scr[par, h]
            else:
                m_old = m_scr[h]
                m_new = jnp.maximum(m_old, cmax_scr[par, h])
            p = jnp.exp2(s_scr[par, h] - m_new)
            p_scr[par, h] = p.astype(BF16)
            l_new = jnp.sum(p, axis=0, keepdims=True)
            if not first:
                alpha = jnp.exp2(m_old - m_new)
                alpha_scr[par, h] = alpha
                l_new = alpha * l_scr[h] + l_new
                if exists is not None:
                    l_new = jnp.where(exists, l_new, l_scr[h])
                    m_new = jnp.where(exists, m_new, m_old)
            l_scr[h] = l_new
            m_scr[h] = m_new

    def stage3(n, par, first, heads=all_heads):
        ks = _key_start(qi, n, tq, lp)
        for h in heads:
            pv = jnp.dot(vt_ref[0, rows(h), pl.ds(ks, ATT_SUB)], p_scr[par, h], preferred_element_type=F32)
            if first:
                acc_scr[rows(h), :] = pv
            else:
                acc_scr[rows(h), :] = alpha_scr[par, h] * acc_scr[rows(h), :] + pv

    _pipeline3(nblk, ndiag, MLA_HEADS, stage1, stage2, stage3)
    for h in range(MLA_HEADS):
        acc_scr[rows(h), :] = acc_scr[rows(h), :] / l_scr[h]
    o_ref[0] = acc_scr[...].T.astype(BF16)


def _mla_attention(qc3, kc3, vt):
    bsz, lp, width = qc3.shape
    tq = ATT_TQ
    out_w = MLA_HEADS * MLA_V
    return pl.pallas_call(
        functools.partial(_mla_kernel, tq=tq, lp=lp),
        grid=(bsz, pl.cdiv(lp, tq)),
        in_specs=[
            pl.BlockSpec((1, tq, width), lambda b, i: (b, i, 0)),
            pl.BlockSpec((1, lp, width), lambda b, i: (b, 0, 0)),
            pl.BlockSpec((1, out_w, lp), lambda b, i: (b, 0, 0)),
        ],
        out_specs=pl.BlockSpec((1, tq, out_w), lambda b, i: (b, i, 0)),
        out_shape=jax.ShapeDtypeStruct((bsz, lp, out_w), BF16),
        scratch_shapes=[pltpu.VMEM((2, MLA_HEADS, ATT_SUB, tq), F32), pltpu.VMEM((2, MLA_HEADS, 1, tq), F32),
                        pltpu.VMEM((2, MLA_HEADS, ATT_SUB, tq), BF16),
                        pltpu.VMEM((2, MLA_HEADS, 1, tq), F32), pltpu.VMEM((out_w, tq), F32),
                        pltpu.VMEM((MLA_HEADS, 1, tq), F32), pltpu.VMEM((MLA_HEADS, 1, tq), F32)],
        compiler_params=_params("parallel", "arbitrary"),
        name="mla_attention",
    )(qc3, kc3, vt)


def _out_ffn_kernel(h_ref, yssd_ref, osb_ref, omla_ref, gsb_ref, gmla_ref, wout_ref, gffn_ref,
                    wup_ref, cw_ref, cb_ref, wdown_ref, o_ref, halo_ref, ext_ref, act_ref, *, tm):
    @pl.when(pl.program_id(1) == 0)
    def _():
        halo_ref[...] = jnp.zeros(halo_ref.shape, F32)

    ysb = _rms(osb_ref[0].astype(F32), gsb_ref[...], SB_WIDTH).astype(BF16)
    ymla = _rms(omla_ref[0].astype(F32), gmla_ref[...], 256).astype(BF16)
    mix = jnp.dot(yssd_ref[0], wout_ref[0:512, :], preferred_element_type=F32)
    mix = mix + jnp.dot(ysb, wout_ref[512:768, :], preferred_element_type=F32)
    mix = mix + jnp.dot(ymla, wout_ref[768:1024, :], preferred_element_type=F32)
    h1 = h_ref[0] + mix
    o_ref[0] = h1
    xn = _rms(h1, gffn_ref[...], D_MODEL).astype(BF16)

    w2 = 2 * FF_CHUNK
    for c in range(D_FF // FF_CHUNK):
        cols = slice(c * w2, (c + 1) * w2)
        ext_ref[0:8, :] = halo_ref[:, cols]
        ext_ref[8:8 + tm, :] = jnp.dot(xn, wup_ref[:, cols], preferred_element_type=F32)
        halo_ref[:, cols] = ext_ref[tm:tm + 8, :]
        cw = cw_ref[:, cols]
        conv = cb_ref[:, cols] + ext_ref[8:8 + tm, :] * cw[2:3]
        conv = conv + ext_ref[7:7 + tm, :] * cw[1:2]
        conv = conv + ext_ref[6:6 + tm, :] * cw[0:1]
        act = _silu(conv[:, :FF_CHUNK]) * conv[:, FF_CHUNK:]
        act_ref[:, c * FF_CHUNK:(c + 1) * FF_CHUNK] = act.astype(BF16)
    o_ref[0] = o_ref[0] + jnp.dot(act_ref[...], wdown_ref[...], preferred_element_type=F32)


def _out_ffn(h3, yssd, osb, omla, gsb, gmla, wout, gffn, wup, cw, cb, wdown):
    bsz, lp, _ = h3.shape
    tm = _pick(lp, (768, 512, 256))
    const = lambda a: pl.BlockSpec(a.shape, lambda b, t: (0, 0), pipeline_mode=pl.Buffered(1))
    rowblk = lambda n: pl.BlockSpec((1, tm, n), lambda b, t: (b, t, 0))
    return pl.pallas_call(
        functools.partial(_out_ffn_kernel, tm=tm),
        grid=(bsz, lp // tm),
        in_specs=[rowblk(D_MODEL), rowblk(SSD_WIDTH), rowblk(256), rowblk(256),
                  const(gsb), const(gmla), const(wout), const(gffn),
                  const(wup), const(cw), const(cb), const(wdown)],
        out_specs=rowblk(D_MODEL),
        out_shape=jax.ShapeDtypeStruct(h3.shape, F32),
        scratch_shapes=[pltpu.VMEM((8, 2 * D_FF), F32),
                        pltpu.VMEM((tm + 8, 2 * FF_CHUNK), F32),
                        pltpu.VMEM((tm, D_FF), BF16)],
        compiler_params=_params("parallel", "arbitrary"),
        name="out_ffn",
    )(h3, yssd, osb, omla, gsb, gmla, wout, gffn, wup, cw, cb, wdown)


def _final_kernel(a_ref, t_ref, g_ref, o_ref, *, tm):
    g = g_ref[...]
    o_ref[0, 0:tm - N_META, :] = _rms(a_ref[0, N_META:tm, :], g, D_MODEL)
    o_ref[0, tm - N_META:tm, :] = _rms(t_ref[0], g, D_MODEL)


def _final_norm(h3, g, seq):
    bsz = h3.shape[0]
    tm = _pick(seq, (512, 128))
    return pl.pallas_call(
        functools.partial(_final_kernel, tm=tm),
        grid=(bsz, seq // tm),
        in_specs=[
            pl.BlockSpec((1, tm, D_MODEL), lambda b, i: (b, i, 0)),
            pl.BlockSpec((1, N_META, D_MODEL), lambda b, i: (b, (i + 1) * (tm // N_META), 0)),
            pl.BlockSpec((1, D_MODEL), lambda b, i: (0, 0)),
        ],
        out_specs=pl.BlockSpec((1, tm, D_MODEL), lambda b, i: (b, i, 0)),
        out_shape=jax.ShapeDtypeStruct((bsz, seq, D_MODEL), F32),
        compiler_params=_params("parallel", "parallel"),
        name="final_norm",
    )(h3, h3, g)


def _rot_cols(w):
    half = w.shape[1] // 2
    return jnp.concatenate([-w[:, half:], w[:, :half]], axis=1)


def _prep_in_proj(w_in):
    zeros = lambda n: jnp.zeros((D_MODEL, n), F32)
    cuts = (512, 1536, 1544, 1800, 2056, 2312, 2504, 2632)
    z, xbc, dt, q_sb, k_sb, v_sb, q_a, c_kv, k_r = jnp.split(w_in, cuts, axis=1)
    kr1 = jnp.concatenate([zeros(MLA_NOPE), k_r, zeros(LANES - MLA_NOPE - MLA_ROPE)], axis=1)
    kr2 = jnp.concatenate([zeros(MLA_NOPE), _rot_cols(k_r), zeros(LANES - MLA_NOPE - MLA_ROPE)], axis=1)
    w = jnp.concatenate([xbc, z, q_sb * (64 ** -0.5), k_sb, v_sb, q_a, zeros(256 - MLA_Q_RANK), c_kv,
                         kr1, kr2, jnp.repeat(dt, SSD_HEAD_DIM, axis=1)], axis=1)
    return w.astype(BF16)


def _prep_mla(w_uq, w_ukv):
    qh = w_uq.reshape(MLA_Q_RANK, MLA_HEADS, MLA_NOPE + MLA_ROPE)
    pad = jnp.zeros((MLA_Q_RANK, MLA_HEADS, LANES - MLA_NOPE - MLA_ROPE), F32)
    nope0 = jnp.zeros((MLA_Q_RANK, MLA_HEADS, MLA_NOPE), F32)
    rope = qh[..., MLA_NOPE:]
    rope_rot = jnp.concatenate([-rope[..., MLA_ROPE // 2:], rope[..., :MLA_ROPE // 2]], axis=-1)
    wq = jnp.concatenate([qh, pad], axis=-1).reshape(MLA_Q_RANK, MLA_HEADS * LANES)
    wqr = jnp.concatenate([nope0, rope_rot, pad], axis=-1).reshape(MLA_Q_RANK, MLA_HEADS * LANES)
    rowpad = jnp.zeros((256 - MLA_Q_RANK, MLA_HEADS * LANES), F32)
    wq = jnp.concatenate([wq, rowpad], axis=0).astype(BF16)
    wqr = jnp.concatenate([wqr, rowpad], axis=0).astype(BF16)
    kvh = w_ukv.reshape(MLA_KV_RANK, MLA_HEADS, MLA_NOPE + 64)
    kpad = jnp.zeros((MLA_KV_RANK, MLA_HEADS, LANES - MLA_NOPE), F32)
    wk = jnp.concatenate([kvh[..., :MLA_NOPE], kpad], axis=-1).reshape(MLA_KV_RANK, MLA_HEADS * LANES)
    wvt = kvh[..., MLA_NOPE:].reshape(MLA_KV_RANK, MLA_HEADS * 64).T
    return wq, wqr, wk.astype(BF16), wvt.astype(BF16)


def _interleave_ff(a):
    lead = a.shape[:-1]
    g = a[..., :D_FF].reshape(lead + (D_FF // FF_CHUNK, FF_CHUNK))
    v = a[..., D_FF:].reshape(lead + (D_FF // FF_CHUNK, FF_CHUNK))
    return jnp.concatenate([g, v], axis=-1).reshape(lead + (2 * D_FF,))


def _rope_tables(lp):
    pos = jnp.arange(lp, dtype=F32)
    inv = 1.0 / (ROPE_BASE ** (jnp.arange(0, MLA_ROPE, 2, dtype=F32) / MLA_ROPE))
    ang = pos[:, None] * inv[None, :]
    ang = jnp.concatenate([ang, ang], axis=-1)
    ones = jnp.ones((lp, MLA_NOPE), F32)
    tail = LANES - MLA_NOPE - MLA_ROPE
    cosx = jnp.concatenate([ones, jnp.cos(ang), jnp.ones((lp, tail), F32)], axis=1)
    sinx = jnp.concatenate([0 * ones, jnp.sin(ang), jnp.zeros((lp, tail), F32)], axis=1)
    return cosx, sinx


def kernel(x, meta_tokens, norm_mix_g, w_in, ssd_conv_w, ssd_conv_b, ssd_dt_bias, ssd_a_log, ssd_d, ssd_norm_g, sb_norm_g, mla_q_norm_g, mla_kv_norm_g, mla_w_uq, mla_w_ukv, mla_norm_g, w_out, norm_ffn_g, ffn_w_up, ffn_conv_w, ffn_conv_b, ffn_w_down, final_norm_g):
    bsz, seq, _ = x.shape
    length = N_META + seq
    lp = -(-length // SEQ_ALIGN) * SEQ_ALIGN
    depth = w_in.shape[0]

    meta = jnp.broadcast_to(meta_tokens[None].astype(x.dtype), (bsz, N_META, D_MODEL))
    h = jnp.concatenate([meta, x, jnp.zeros((bsz, lp - length, D_MODEL), x.dtype)], axis=1)

    cosx, sinx = _rope_tables(lp)
    later_t = (lax.broadcasted_iota(jnp.int32, (ATT_SUB, ATT_SUB), 1) >=
               lax.broadcasted_iota(jnp.int32, (ATT_SUB, ATT_SUB), 0)).astype(BF16)
    row = lambda v: v.reshape(1, -1)
    rep = lambda v: jnp.repeat(v, SSD_HEAD_DIM).reshape(1, -1)

    for l in range(depth):
        u, dtx = _in_proj(h.reshape(bsz * lp, D_MODEL), row(norm_mix_g[l]), _prep_in_proj(w_in[l]))
        u3 = u.reshape(bsz, lp, U_END)
        y_ssd = _ssd(u3, dtx.reshape(bsz, lp, SSD_WIDTH), ssd_conv_w[l], row(ssd_conv_b[l]),
                     rep(ssd_dt_bias[l]), rep(ssd_a_log[l]), rep(ssd_d[l]), row(ssd_norm_g[l]))
        vt_sb = jnp.swapaxes(u3[:, :, U_VSB:U_VSB + SB_WIDTH], 1, 2)
        o_sb = _sb_attention(u3, vt_sb, later_t)
        wq, wqr, wk, wvt = _prep_mla(mla_w_uq[l], mla_w_ukv[l])
        gq = jnp.concatenate([mla_q_norm_g[l], jnp.zeros((256 - MLA_Q_RANK,), F32)]).reshape(1, -1)
        qc, kc, vt_mla = _mla_proj(u, cosx, sinx, gq, row(mla_kv_norm_g[l]), wq, wqr, wk, wvt, bsz, lp)
        o_mla = _mla_attention(qc.reshape(bsz, lp, -1), kc.reshape(bsz, lp, -1), vt_mla)
        h = _out_ffn(h, y_ssd, o_sb, o_mla, row(sb_norm_g[l]), row(mla_norm_g[l]),
                     w_out[l].astype(BF16), row(norm_ffn_g[l]),
                     _interleave_ff(ffn_w_up[l]).astype(BF16), _interleave_ff(ffn_conv_w[l]),
                     row(_interleave_ff(ffn_conv_b[l])), ffn_w_down[l].astype(BF16))

    return _final_norm(h, row(final_norm_g), seq)
```
